```python
import math
import jax, jax.numpy as jnp
from jax import lax
import numpy as np

D_MODEL = 4096
BATCH = 2
SEQ = 8192
DEPTH = 2

MIX_WIDTH = D_MODEL
RWKV_WIDTH = MIX_WIDTH // 2
S5_WIDTH = MIX_WIDTH - RWKV_WIDTH
RWKV_HEAD_DIM = 64
RWKV_HEADS = RWKV_WIDTH // RWKV_HEAD_DIM
DECAY_LORA = max(32, int(round(1.8 * RWKV_WIDTH ** 0.5 / 32)) * 32)
AAA_LORA = max(32, int(round(1.8 * RWKV_WIDTH ** 0.5 / 32)) * 32)
GATE_LORA = max(32, int(round(0.6 * RWKV_WIDTH ** 0.8 / 32)) * 32)
LN_X_EPS = 64e-5
S5_GROUP_CH = 16
S5_GROUPS = S5_WIDTH // S5_GROUP_CH
S5_STATE = 64
D_FF = 4 * D_MODEL
PLE_DIM = 256
RMS_EPS = 1e-6

R_OFF = 0
K_OFF = R_OFF + RWKV_WIDTH
V_OFF = K_OFF + RWKV_WIDTH
WL_OFF = V_OFF + RWKV_WIDTH
AL_OFF = WL_OFF + DECAY_LORA
GL_OFF = AL_OFF + AAA_LORA
RWKV_COLS = GL_OFF + GATE_LORA
IN_COLS = RWKV_COLS + S5_WIDTH

kernel_name = "hybrid_rwkv7_s5_sandwich_block"


def rms_norm(x, g):
    xf = x.astype(jnp.float32)
    y = xf * lax.rsqrt(jnp.mean(xf * xf, axis=-1, keepdims=True) + RMS_EPS)
    return (y * g.astype(jnp.float32)).astype(x.dtype)


def token_shift(z):
    return jnp.pad(z[:, :-1], ((0, 0), (1, 0), (0, 0)))


def _rwkv7_recurrence(r, w, k, v, a, b):
    bsz, _, nh, nd = r.shape

    def step(S, inp):
        r_t, w_t, k_t, v_t, a_t, b_t = inp
        sa = jnp.einsum('bhij,bhj->bhi', S, a_t)
        S = (S * w_t[:, :, None, :] + sa[..., None] * b_t[:, :, None, :]
             + v_t[..., None] * k_t[:, :, None, :])
        y_t = jnp.einsum('bhij,bhj->bhi', S, r_t)
        return S, y_t

    seqs = tuple(jnp.moveaxis(t, 1, 0) for t in (r, w, k, v, a, b))
    S0 = jnp.zeros((bsz, nh, nd, nd), jnp.float32)
    _, y = lax.scan(step, S0, seqs)
    return jnp.moveaxis(y, 0, 1)


def rwkv7_mixer(zr, mu, w0, w2, a0, a2, g2, k_k, k_a, r_k, lnx_w, lnx_b):
    bsz, seq, _ = zr.shape
    zf = zr.astype(jnp.float32)
    zf = zf + (token_shift(zf) - zf) * mu.astype(jnp.float32)
    r = zf[..., R_OFF:K_OFF]
    k = zf[..., K_OFF:V_OFF]
    v = zf[..., V_OFF:WL_OFF]
    xw = zf[..., WL_OFF:AL_OFF]
    xa = zf[..., AL_OFF:GL_OFF]
    xg = zf[..., GL_OFF:RWKV_COLS]
    w_raw = -jax.nn.softplus(-(w0 + jnp.tanh(xw) @ w2)) - 0.5
    decay = jnp.exp(-jnp.exp(w_raw))
    a = jax.nn.sigmoid(a0 + xa @ a2)
    g = jax.nn.sigmoid(xg) @ g2

    def heads(t):
        return t.reshape(bsz, seq, RWKV_HEADS, RWKV_HEAD_DIM)

    kk = heads(k * k_k)
    kk = kk / jnp.maximum(jnp.sqrt(jnp.sum(kk * kk, axis=-1, keepdims=True)), 1e-12)
    k = k * (1.0 + (a - 1.0) * k_a)
    rh, kh, vh, ah = heads(r), heads(k), heads(v), heads(a)
    y = _rwkv7_recurrence(rh, heads(decay), kh, vh, -kk, kk * ah)
    mean = jnp.mean(y, axis=-1, keepdims=True)
    var = jnp.mean(jnp.square(y - mean), axis=-1, keepdims=True)
    y = ((y - mean) * lax.rsqrt(var + LN_X_EPS)).reshape(bsz, seq, RWKV_WIDTH) * lnx_w + lnx_b
    bonus = jnp.sum(rh * kh * r_k, axis=-1, keepdims=True) * vh
    y = (y + bonus.reshape(bsz, seq, RWKV_WIDTH)) * g
    return y.astype(zr.dtype)


def _complex_linear_combine(left, right):
    ar1, ai1, br1, bi1 = left
    ar2, ai2, br2, bi2 = right
    ar = ar2 * ar1 - ai2 * ai1
    ai = ar2 * ai1 + ai2 * ar1
    br = ar2 * br1 - ai2 * bi1 + br2
    bi = ar2 * bi1 + ai2 * br1 + bi2
    return (ar, ai, br, bi)


def s5_mixer(u, lam_re, lam_im, log_step, b_re, b_im, c_re, c_im, d_skip, w_glu, b_glu):
    bsz, seq, _ = u.shape
    uf = u.astype(jnp.float32)
    ug = uf.reshape(bsz, seq, S5_GROUPS, S5_GROUP_CH)
    lr = lam_re.astype(jnp.float32)
    li = lam_im.astype(jnp.float32)
    dt = jnp.exp(log_step.astype(jnp.float32))[:, None]
    mag = jnp.exp(lr * dt)
    abar_re = mag * jnp.cos(li * dt)
    abar_im = mag * jnp.sin(li * dt)
    den = lr * lr + li * li
    q_re = ((abar_re - 1.0) * lr + abar_im * li) / den
    q_im = (abar_im * lr - (abar_re - 1.0) * li) / den
    bu_re = jnp.einsum('btgh,gph->btgp', ug, b_re)
    bu_im = jnp.einsum('btgh,gph->btgp', ug, b_im)
    bx_re = q_re * bu_re - q_im * bu_im
    bx_im = q_re * bu_im + q_im * bu_re
    a_re = jnp.broadcast_to(abar_re, (1, seq, S5_GROUPS, S5_STATE))
    a_im = jnp.broadcast_to(abar_im, (1, seq, S5_GROUPS, S5_STATE))
    _, _, s_re, s_im = lax.associative_scan(
        _complex_linear_combine, (a_re, a_im, bx_re, bx_im), axis=1)
    y = (jnp.einsum('btgp,ghp->btgh', s_re, c_re)
         - jnp.einsum('btgp,ghp->btgh', s_im, c_im)
         + d_skip.reshape(S5_GROUPS, S5_GROUP_CH) * ug)
    y = jax.nn.gelu(y.reshape(bsz, seq, S5_WIDTH))
    out = y * jax.nn.sigmoid(y @ w_glu + b_glu)
    return out.astype(u.dtype)


def setup_inputs(seed: int = 0) -> dict:
    key = jax.random.key(seed)
    keys = jax.random.split(key, 40)
    counter = iter(range(40))
    L = DEPTH
    f32 = jnp.float32

    def nk():
        return keys[next(counter)]

    def nrm(shape, scale):
        return scale * jax.random.normal(nk(), shape, f32)

    def gain(shape):
        return 1.0 + nrm(shape, 0.02)

    x = nrm((BATCH, SEQ, D_MODEL), 1.0)
    p = nrm((DEPTH, BATCH, SEQ, PLE_DIM), 1.0)
    g_mix_pre = gain((L, D_MODEL))
    w_in = nrm((L, D_MODEL, IN_COLS), D_MODEL ** -0.5)
    mu = jax.random.uniform(nk(), (L, RWKV_COLS), f32, 0.2, 0.8)
    ratio = jnp.linspace(0.0, 1.0, RWKV_WIDTH, dtype=f32)
    w0 = -6.0 + 5.0 * ratio ** 0.85 + nrm((L, RWKV_WIDTH), 0.01)
    w2 = nrm((L, DECAY_LORA, RWKV_WIDTH), 0.1 * DECAY_LORA ** -0.5)
    a0 = nrm((L, RWKV_WIDTH), 0.1)
    a2 = nrm((L, AAA_LORA, RWKV_WIDTH), 0.1 * AAA_LORA ** -0.5)
    g2 = nrm((L, GATE_LORA, RWKV_WIDTH), GATE_LORA ** -0.5)
    k_k = 0.85 + nrm((L, RWKV_WIDTH), 0.02)
    k_a = 1.0 + nrm((L, RWKV_WIDTH), 0.02)
    r_k = nrm((L, RWKV_HEADS, RWKV_HEAD_DIM), 0.1)
    lnx_w = gain((L, RWKV_WIDTH))
    lnx_b = nrm((L, RWKV_WIDTH), 0.02)
    lam_re = -0.5 + nrm((L, S5_GROUPS, S5_STATE), 0.01)
    lam_im = jnp.pi * jnp.arange(S5_STATE, dtype=f32) + nrm((L, S5_GROUPS, S5_STATE), 0.01)
    log_step = jax.random.uniform(nk(), (L, S5_GROUPS), f32, math.log(1e-3), math.log(1e-1))
    b_re = nrm((L, S5_GROUPS, S5_STATE, S5_GROUP_CH), (2 * S5_GROUP_CH) ** -0.5)
    b_im = nrm((L, S5_GROUPS, S5_STATE, S5_GROUP_CH), (2 * S5_GROUP_CH) ** -0.5)
    c_re = nrm((L, S5_GROUPS, S5_GROUP_CH, S5_STATE), 0.5)
    c_im = nrm((L, S5_GROUPS, S5_GROUP_CH, S5_STATE), 0.5)
    d_skip = nrm((L, S5_WIDTH), 1.0)
    w_glu = nrm((L, S5_WIDTH, S5_WIDTH), S5_WIDTH ** -0.5)
    b_glu = nrm((L, S5_WIDTH), 0.02)
    w_out = nrm((L, MIX_WIDTH, D_MODEL), MIX_WIDTH ** -0.5)
    g_mix_post = gain((L, D_MODEL))
    g_ffn_pre = gain((L, D_MODEL))
    w_ff1 = nrm((L, D_MODEL, D_FF), D_MODEL ** -0.5)
    w_ff2 = nrm((L, D_FF, D_MODEL), D_FF ** -0.5)
    g_ffn_post = gain((L, D_MODEL))
    w_ple = nrm((L, PLE_DIM, D_MODEL), PLE_DIM ** -0.5)
    g_ple_gate = gain((L, D_MODEL))
    w_ple_gate = nrm((L, D_MODEL, D_MODEL), D_MODEL ** -0.5)
    g_ple_post = gain((L, D_MODEL))
    return {"x": x, "p": p, "g_mix_pre": g_mix_pre, "w_in": w_in, "mu": mu,
            "w0": w0, "w2": w2, "a0": a0, "a2": a2, "g2": g2, "k_k": k_k, "k_a": k_a,
            "r_k": r_k, "lnx_w": lnx_w, "lnx_b": lnx_b, "lam_re": lam_re, "lam_im": lam_im,
            "log_step": log_step, "b_re": b_re, "b_im": b_im, "c_re": c_re, "c_im": c_im,
            "d_skip": d_skip, "w_glu": w_glu, "b_glu": b_glu, "w_out": w_out,
            "g_mix_post": g_mix_post, "g_ffn_pre": g_ffn_pre, "w_ff1": w_ff1, "w_ff2": w_ff2,
            "g_ffn_post": g_ffn_post, "w_ple": w_ple, "g_ple_gate": g_ple_gate,
            "w_ple_gate": w_ple_gate, "g_ple_post": g_ple_post}


def reference(x, p, g_mix_pre, w_in, mu, w0, w2, a0, a2, g2, k_k, k_a, r_k, lnx_w, lnx_b,
              lam_re, lam_im, log_step, b_re, b_im, c_re, c_im, d_skip, w_glu, b_glu,
              w_out, g_mix_post, g_ffn_pre, w_ff1, w_ff2, g_ffn_post, w_ple, g_ple_gate,
              w_ple_gate, g_ple_post):
    h = x
    for i in range(DEPTH):
        hn = rms_norm(h, g_mix_pre[i])
        z = hn @ w_in[i]
        y_rwkv = rwkv7_mixer(z[..., :RWKV_COLS], mu[i], w0[i], w2[i], a0[i], a2[i], g2[i],
                             k_k[i], k_a[i], r_k[i], lnx_w[i], lnx_b[i])
        y_s5 = s5_mixer(z[..., RWKV_COLS:], lam_re[i], lam_im[i], log_step[i], b_re[i],
                        b_im[i], c_re[i], c_im[i], d_skip[i], w_glu[i], b_glu[i])
        mixed = jnp.concatenate([y_rwkv, y_s5], axis=-1) @ w_out[i]
        h = h + rms_norm(mixed, g_mix_post[i])
        hn = rms_norm(h, g_ffn_pre[i])
        f = jnp.square(jax.nn.relu(hn @ w_ff1[i])) @ w_ff2[i]
        h = h + rms_norm(f, g_ffn_post[i])
        e = p[i] @ w_ple[i]
        gate = jax.nn.sigmoid(rms_norm(h, g_ple_gate[i]) @ w_ple_gate[i])
        h = h + rms_norm(gate * e, g_ple_post[i])
    return h
```

```python
import functools
import math

import jax
import jax.numpy as jnp
from jax import lax
from jax.experimental import pallas as pl
from jax.experimental.pallas import tpu as pltpu

F32 = jnp.float32
BF16 = jnp.bfloat16

V7X_VMEM_BYTES = 64 * 1024 * 1024
LANES = 128
MXU_DIM = 256

VMEM_LIMIT = V7X_VMEM_BYTES - 8 * 1024 * 1024

RMS_EPS = 1e-6
LN_X_EPS = 64e-5
HEAD_DIM = 64
S5_GROUP_CH = 16
S5_CHUNK = 16
RWKV_CHUNK = 64


def _bf(x):
    return x.astype(BF16)


def _dot(a, b):
    return jnp.dot(a, b, preferred_element_type=F32)


def _dot_nt(a, b):
    return lax.dot_general(a, b, (((1,), (1,)), ((), ())), preferred_element_type=F32)


def _dot_tn(a, b):
    return lax.dot_general(a, b, (((0,), (0,)), ((), ())), preferred_element_type=F32)


def _sigmoid(x):
    return 1.0 / (1.0 + jnp.exp(-x))


def _cparams(sem):
    return pltpu.CompilerParams(dimension_semantics=sem, vmem_limit_bytes=VMEM_LIMIT)


def _pick(n, cands):
    for c in cands:
        if n % c == 0:
            return c
    return n


def _mm_ares_kernel(*refs, norm, epilogue):
    if epilogue == "glu":
        a_ref, g_ref, w_ref, y_ref, b_ref, o_ref, a_scr = refs
    elif epilogue == "ple":
        a_ref, g_ref, w_ref, p_ref, wp_ref, o_ref, a_scr = refs
    else:
        a_ref, g_ref, w_ref, o_ref, a_scr = refs

    @pl.when(pl.program_id(1) == 0)
    def _():
        x = a_ref[...].astype(F32)
        if norm:
            ms = jnp.mean(x * x, axis=-1, keepdims=True)
            x = (x * lax.rsqrt(ms + RMS_EPS)) * g_ref[...]
        a_scr[...] = _bf(x)

    acc = _dot(a_scr[...], w_ref[...])
    if epilogue == "none":
        o_ref[...] = acc.astype(o_ref.dtype)
    elif epilogue == "relu2":
        r = jnp.maximum(acc, 0.0)
        o_ref[...] = (r * r).astype(o_ref.dtype)
    elif epilogue == "glu":
        o_ref[...] = (y_ref[...] * _sigmoid(acc + b_ref[...])).astype(o_ref.dtype)
    elif epilogue == "ple":
        e = _dot(p_ref[...], wp_ref[...])
        o_ref[...] = (_sigmoid(acc) * e).astype(o_ref.dtype)


def _mm_ares(a, g, w, *, norm, epilogue, out_dtype, extra=(), name):
    m, k = a.shape
    n = w.shape[1]
    tm = _pick(m, (512, 256, 128))
    tn = _pick(n, (1024, 512, 256, 128)) if a.dtype == BF16 or k <= 2048 else _pick(n, (512, 256, 128))
    in_specs = [
        pl.BlockSpec((tm, k), lambda i, j: (i, 0)),
        pl.BlockSpec((1, k), lambda i, j: (0, 0)),
        pl.BlockSpec((k, tn), lambda i, j: (0, j)),
    ]
    args = [a, g.reshape(1, k).astype(F32), w]
    if epilogue == "glu":
        y, b = extra
        in_specs += [pl.BlockSpec((tm, tn), lambda i, j: (i, j)),
                     pl.BlockSpec((1, tn), lambda i, j: (0, j))]
        args += [y, b.reshape(1, n).astype(F32)]
    elif epilogue == "ple":
        p, wp = extra
        kp = p.shape[1]
        in_specs += [pl.BlockSpec((tm, kp), lambda i, j: (i, 0)),
                     pl.BlockSpec((kp, tn), lambda i, j: (0, j))]
        args += [p, wp]
    return pl.pallas_call(
        functools.partial(_mm_ares_kernel, norm=norm, epilogue=epilogue),
        grid=(m // tm, n // tn),
        in_specs=in_specs,
        out_specs=pl.BlockSpec((tm, tn), lambda i, j: (i, j)),
        out_shape=jax.ShapeDtypeStruct((m, n), out_dtype),
        scratch_shapes=[pltpu.VMEM((tm, k), BF16)],
        compiler_params=_cparams(("parallel", "arbitrary")),
        name=name,
    )(*args)


def _mm_kt_kernel(a_ref, w_ref, o_ref):
    prod = _dot(a_ref[...], w_ref[...])

    @pl.when(pl.program_id(1) == 0)
    def _():
        o_ref[...] = prod

    @pl.when(pl.program_id(1) > 0)
    def _():
        o_ref[...] += prod


def _mm_kt(a, w, *, name):
    m, k = a.shape
    n = w.shape[1]
    tm = _pick(m, (512, 256, 128))
    tk = _pick(k, (512, 256, 128))
    return pl.pallas_call(
        _mm_kt_kernel,
        grid=(m // tm, k // tk),
        in_specs=[pl.BlockSpec((tm, tk), lambda i, kk: (i, kk)),
                  pl.BlockSpec((tk, n), lambda i, kk: (kk, 0))],
        out_specs=pl.BlockSpec((tm, n), lambda i, kk: (i, 0)),
        out_shape=jax.ShapeDtypeStruct((m, n), F32),
        compiler_params=_cparams(("parallel", "arbitrary")),
        name=name,
    )(a, w)


def _resid_norm_kernel(h_ref, f_ref, g_ref, o_ref):
    f = f_ref[...]
    ms = jnp.mean(f * f, axis=-1, keepdims=True)
    o_ref[...] = h_ref[...] + (f * lax.rsqrt(ms + RMS_EPS)) * g_ref[...]


def _resid_norm(h, f, g, *, name):
    m, d = h.shape
    tm = _pick(m, (256, 128))
    spec = pl.BlockSpec((tm, d), lambda i: (i, 0))
    return pl.pallas_call(
        _resid_norm_kernel,
        grid=(m // tm,),
        in_specs=[spec, spec, pl.BlockSpec((1, d), lambda i: (0, 0))],
        out_specs=spec,
        out_shape=jax.ShapeDtypeStruct((m, d), F32),
        compiler_params=_cparams(("parallel",)),
        name=name,
    )(h, f, g.reshape(1, d).astype(F32))


def _segsum(x, ones_bd):
    hi = _bf(x)
    lo = _bf(x - hi.astype(F32))
    outs = []
    for s in range(x.shape[1] // MXU_DIM):
        sl = slice(s * MXU_DIM, (s + 1) * MXU_DIM)
        outs.append(_dot(hi[:, sl], ones_bd) + _dot(lo[:, sl], ones_bd))
    return jnp.concatenate(outs, axis=1)


def _rwkv_kernel(zm_ref, zl_ref, mum_ref, mul_ref, w0_ref, w2_ref, a0_ref, a2_ref, g2_ref,
                 kk_ref, ka_ref, rk_ref, lw_ref, lb_ref,
                 o_ref,
                 st_ref, carm_ref, carl_ref,
                 at_s, rt_s, bt_s, kt_s, bh_s, kh_s, v_s, pe_s, y_s,
                 *, tt, width, dlp, alp):
    c_len = RWKV_CHUNK
    n = HEAD_DIM
    heads = width // n

    @pl.when(pl.program_id(1) == 0)
    def _():
        st_ref[...] = jnp.zeros_like(st_ref)
        carm_ref[...] = jnp.zeros_like(carm_ref)
        carl_ref[...] = jnp.zeros_like(carl_ref)

    row = lax.broadcasted_iota(jnp.int32, (tt, 1), 0)

    def shift_lerp(x, car_ref, mu):
        xs = pltpu.roll(x, 1, axis=0)
        xs = jnp.where(row == 0, car_ref[...], xs)
        car_ref[...] = x[tt - 1:tt, :]
        return x + (xs - x) * mu

    zm = shift_lerp(zm_ref[...], carm_ref, mum_ref[...])
    zl = shift_lerp(zl_ref[...], carl_ref, mul_ref[...])
    r = zm[:, :width]
    k = zm[:, width:2 * width]
    v = zm[:, 2 * width:]
    xw = zl[:, :dlp]
    xa = zl[:, dlp:dlp + alp]
    xg = zl[:, dlp + alp:]

    wr = w0_ref[...] + _dot(_bf(jnp.tanh(xw)), w2_ref[...])
    softplus_neg = jnp.maximum(-wr, 0.0) + jnp.log(1.0 + jnp.exp(-jnp.abs(wr)))
    wlog = -jnp.exp(-softplus_neg - 0.5)
    a = _sigmoid(a0_ref[...] + _dot(_bf(xa), a2_ref[...]))
    g = _dot(_bf(_sigmoid(xg)), g2_ref[...])

    ri = lax.broadcasted_iota(jnp.int32, (MXU_DIM, MXU_DIM), 0)
    ci = lax.broadcasted_iota(jnp.int32, (MXU_DIM, MXU_DIM), 1)
    shift = int(math.log2(n))
    ones_bd = jnp.where((ri >> shift) == (ci >> shift), 1.0, 0.0).astype(BF16)

    kkv = k * kk_ref[...]
    ssq = _segsum(kkv * kkv, ones_bd)
    kkn = kkv / jnp.maximum(jnp.sqrt(ssq), 1e-12)
    k2 = k * (1.0 + (a - 1.0) * ka_ref[...])
    beta = kkn * a
    bonus = _segsum(r * k2 * rk_ref[...], ones_bd) * v

    rt_i = lax.broadcasted_iota(jnp.int32, (tt, tt), 0)
    ct_i = lax.broadcasted_iota(jnp.int32, (tt, tt), 1)
    cshift = int(math.log2(c_len))
    same = (rt_i >> cshift) == (ct_i >> cshift)
    lt_bd = jnp.where(same & (ct_i <= rt_i), 1.0, 0.0).astype(BF16)
    le_bd = jnp.where(same, 1.0, 0.0).astype(BF16)
    w_hi = _bf(wlog)
    w_lo = _bf(wlog - w_hi.astype(F32))
    cum = _dot(lt_bd, w_hi) + _dot(lt_bd, w_lo)
    cend = _dot(le_bd, w_hi) + _dot(le_bd, w_lo)

    inv = jnp.exp(-cum)
    dend = jnp.exp(cend - cum)
    at_s[...] = _bf(-kkn * jnp.exp(cum - wlog))
    rt_s[...] = _bf(r * jnp.exp(cum))
    bt_s[...] = _bf(beta * inv)
    kt_s[...] = _bf(k2 * inv)
    bh_s[...] = _bf(beta * dend)
    kh_s[...] = _bf(k2 * dend)
    v_s[...] = _bf(v)
    pe_s[...] = jnp.exp(cend)

    tr = lax.broadcasted_iota(jnp.int32, (c_len, c_len), 0)
    tc = lax.broadcasted_iota(jnp.int32, (c_len, c_len), 1)
    strict = tc < tr
    incl = tc <= tr
    eye = jnp.where(tc == tr, 1.0, 0.0).astype(F32)
    n_doublings = int(math.log2(c_len)) - 1

    for c in range(tt // c_len):
        rows = slice(c * c_len, (c + 1) * c_len)

        def pair_body(p, carry, rows=rows, c=c):
            lo = pl.multiple_of(p * LANES, LANES)
            lanes = pl.ds(lo, LANES)
            at_p = at_s[rows, lanes]
            rt_p = rt_s[rows, lanes]
            bt_p = bt_s[rows, lanes]
            kt_p = kt_s[rows, lanes]
            bh_p = bh_s[rows, lanes]
            kh_p = kh_s[rows, lanes]
            v_p = v_s[rows, lanes]
            pe_p = pe_s[c * c_len:c * c_len + 1, lanes]
            outs = []
            for q in range(LANES // n):
                sl = slice(q * n, (q + 1) * n)
                a_q, r_q, b_q, k_q = at_p[:, sl], rt_p[:, sl], bt_p[:, sl], kt_p[:, sl]
                bh_q, kh_q, v_q = bh_p[:, sl], kh_p[:, sl], v_p[:, sl]
                hidx = p * (LANES // n) + q
                s_old = st_ref[hidx]
                sb = _bf(s_old)
                l_ab = jnp.where(strict, _dot_nt(a_q, b_q), 0.0)
                l_ak = jnp.where(strict, _dot_nt(a_q, k_q), 0.0)
                l_rb = jnp.where(incl, _dot_nt(r_q, b_q), 0.0)
                l_rk = jnp.where(incl, _dot_nt(r_q, k_q), 0.0)
                x = _dot_nt(a_q, sb) + _dot(_bf(l_ak), v_q)
                inv_m = eye + l_ab
                lp = l_ab
                for _ in range(n_doublings):
                    lpb = _bf(lp)
                    lp = _dot(lpb, lpb)
                    inv_m = inv_m + _dot(_bf(inv_m), _bf(lp))
                u = _dot(_bf(inv_m), _bf(x))
                ub = _bf(u)
                y = _dot_nt(r_q, sb) + _dot(_bf(l_rb), ub) + _dot(_bf(l_rk), v_q)
                st_ref[hidx] = s_old * pe_p[:, sl] + _dot_tn(ub, bh_q) + _dot_tn(v_q, kh_q)
                outs.append(y)
            y_s[rows, lanes] = jnp.concatenate(outs, axis=1)
            return carry

        lax.fori_loop(0, heads // (LANES // n), pair_body, 0)

    y = y_s[...]
    mean = _segsum(y, ones_bd) * (1.0 / n)
    yc = y - mean
    var = _segsum(yc * yc, ones_bd) * (1.0 / n)
    yn = yc * lax.rsqrt(var + LN_X_EPS) * lw_ref[...] + lb_ref[...]
    o_ref[...] = ((yn + bonus) * g).astype(o_ref.dtype)


def _rwkv(z, mu_m, mu_l, w0, w2p, a0, a2p, g2p, k_k, k_a, r_k, lnx_w, lnx_b, *, width, s5w, name):
    b, t, zc = z.shape
    dlp, alp, glp = w2p.shape[0], a2p.shape[0], g2p.shape[0]
    lw = dlp + alp + glp
    tt = _pick(t, (128, 64))
    lora_blk = (3 * width + s5w) // lw
    assert lora_blk * lw == 3 * width + s5w
    heads = width // HEAD_DIM

    def row(x):
        return x.reshape(1, -1).astype(F32)

    vec = pl.BlockSpec((1, width), lambda bi, i: (0, 0))
    kern = functools.partial(_rwkv_kernel, tt=tt, width=width, dlp=dlp, alp=alp)
    return pl.pallas_call(
        kern,
        grid=(b, t // tt),
        in_specs=[
            pl.BlockSpec((None, tt, 3 * width), lambda bi, i: (bi, i, 0)),
            pl.BlockSpec((None, tt, lw), lambda bi, i: (bi, i, lora_blk)),
            pl.BlockSpec((1, 3 * width), lambda bi, i: (0, 0)),
            pl.BlockSpec((1, lw), lambda bi, i: (0, 0)),
            vec,
            pl.BlockSpec((dlp, width), lambda bi, i: (0, 0)),
            vec,
            pl.BlockSpec((alp, width), lambda bi, i: (0, 0)),
            pl.BlockSpec((glp, width), lambda bi, i: (0, 0)),
            vec, vec, vec, vec, vec,
        ],
        out_specs=pl.BlockSpec((None, tt, width), lambda bi, i: (bi, i, 0)),
        out_shape=jax.ShapeDtypeStruct((b, t, width), BF16),
        scratch_shapes=[
            pltpu.VMEM((heads, HEAD_DIM, HEAD_DIM), F32),
            pltpu.VMEM((1, 3 * width), F32),
            pltpu.VMEM((1, lw), F32),
        ] + [pltpu.VMEM((tt, width), BF16)] * 7 + [pltpu.VMEM((tt, width), F32)] * 2,
        compiler_params=_cparams(("parallel", "arbitrary")),
        name=name,
    )(z, z, row(mu_m), row(mu_l), row(w0), w2p, row(a0), a2p, g2p,
      row(k_k), row(k_a), row(r_k), row(lnx_w), row(lnx_b))


def _gelu_tanh(x):
    return x * (0.5 * (1.0 + jnp.tanh(0.7978845608028654 * (x + 0.044715 * (x * x * x)))))


def _s5_kernel(u_ref, t_ref, w_ref, g_ref, m1_ref, m2_ref, o_ref, *, gb, nc, nlev):
    rows = u_ref.shape[1]
    cidx = lax.broadcasted_iota(jnp.int32, (rows, 1), 0) & (nc - 1)
    for gi in range(gb):
        u = u_ref[gi]
        y_intra = _dot(u, t_ref[gi])
        x = _dot(u, w_ref[gi])
        half = x.shape[1] // 2
        for lev in range(nlev):
            sh = 1 << lev
            xs = jnp.where(cidx >= sh, pltpu.roll(x, sh, axis=0), 0.0)
            xsw = pltpu.roll(xs, half, axis=1)
            x = x + m1_ref[gi, lev:lev + 1, :] * xs + m2_ref[gi, lev:lev + 1, :] * xsw
        xp = jnp.where(cidx >= 1, pltpu.roll(x, 1, axis=0), 0.0)
        y = y_intra + _dot(_bf(xp), g_ref[gi])
        o_ref[gi] = _gelu_tanh(y)


def _s5_core(u_g, t_m, w_m, g_m, m1, m2, *, nc, name):
    groups, rows, lc = u_g.shape
    sp = w_m.shape[2]
    nlev = m1.shape[1]
    gb = _pick(groups, (8, 4, 2, 1))

    def spec(shape):
        return pl.BlockSpec((gb,) + shape, lambda i: (i, 0, 0))

    return pl.pallas_call(
        functools.partial(_s5_kernel, gb=gb, nc=nc, nlev=nlev),
        grid=(groups // gb,),
        in_specs=[spec((rows, lc)), spec((lc, lc)), spec((lc, sp)), spec((sp, lc)),
                  spec((nlev, sp)), spec((nlev, sp))],
        out_specs=spec((rows, lc)),
        out_shape=jax.ShapeDtypeStruct((groups, rows, lc), F32),
        compiler_params=_cparams(("parallel",)),
        name=name,
    )(u_g, t_m, w_m, g_m, m1, m2)


def _s5_operators(lam_re, lam_im, log_step, b_re, b_im, c_re, c_im, d_skip, *, nc):
    groups, sp = lam_re.shape
    ch = b_re.shape[2]
    lc = S5_CHUNK
    lr = lam_re.astype(F32)
    li = lam_im.astype(F32)
    dt = jnp.exp(log_step.astype(F32))[:, None]

    def apow(nsteps):
        nsteps = jnp.asarray(nsteps, F32)[:, None, None]
        mag = jnp.exp(lr * dt * nsteps)
        ang = li * dt * nsteps
        return mag * jnp.cos(ang), mag * jnp.sin(ang)

    ar, ai = apow(jnp.arange(lc + 1))
    den = lr * lr + li * li
    q_re = ((ar[1] - 1.0) * lr + ai[1] * li) / den
    q_im = (ai[1] * lr - (ar[1] - 1.0) * li) / den
    bb_re = q_re[:, :, None] * b_re - q_im[:, :, None] * b_im
    bb_im = q_re[:, :, None] * b_im + q_im[:, :, None] * b_re
    ce_re = c_re[None] * ar[:, :, None, :] - c_im[None] * ai[:, :, None, :]
    ce_im = c_re[None] * ai[:, :, None, :] + c_im[None] * ar[:, :, None, :]
    hp = lax.Precision.HIGHEST
    taps = (jnp.einsum('tghp,gpk->tghk', ce_re[:lc], bb_re, precision=hp)
            - jnp.einsum('tghp,gpk->tghk', ce_im[:lc], bb_im, precision=hp))
    taps = jnp.concatenate([taps, jnp.zeros_like(taps[:1])], axis=0)
    s_i = jnp.arange(lc)[:, None]
    t_i = jnp.arange(lc)[None, :]
    lag = jnp.where(t_i >= s_i, t_i - s_i, lc)
    t5 = taps[lag]
    t_m = jnp.transpose(t5, (2, 0, 4, 1, 3))
    skip = (jnp.eye(lc, dtype=F32)[None, :, None, :, None]
            * jnp.eye(ch, dtype=F32)[None, None, :, None, :]
            * d_skip.reshape(groups, 1, 1, 1, ch).astype(F32))
    t_m = (t_m + skip).reshape(groups, lc * ch, lc * ch)
    arr = ar[lc - 1 - jnp.arange(lc)]
    air = ai[lc - 1 - jnp.arange(lc)]
    w_re = arr[:, :, :, None] * bb_re[None] - air[:, :, :, None] * bb_im[None]
    w_im = arr[:, :, :, None] * bb_im[None] + air[:, :, :, None] * bb_re[None]
    w_m = jnp.concatenate([w_re, w_im], axis=2)
    w_m = jnp.transpose(w_m, (1, 0, 3, 2)).reshape(groups, lc * ch, 2 * sp)
    g_re = jnp.transpose(ce_re[1:], (1, 3, 0, 2))
    g_im = jnp.transpose(ce_im[1:], (1, 3, 0, 2))
    g_m = jnp.concatenate([g_re, -g_im], axis=1).reshape(groups, 2 * sp, lc * ch)
    nlev = max(1, int(math.ceil(math.log2(nc))))
    mr, mi = apow(lc * (2.0 ** jnp.arange(nlev)))
    m1 = jnp.transpose(jnp.concatenate([mr, mr], axis=2), (1, 0, 2))
    m2 = jnp.transpose(jnp.concatenate([-mi, mi], axis=2), (1, 0, 2))
    return _bf(t_m), _bf(w_m), _bf(g_m), m1, m2


def _pad_rows(w, to):
    return jnp.pad(w, ((0, to - w.shape[0]), (0, 0)))


def _round_up(x, m):
    return (x + m - 1) // m * m


def kernel(x, p, g_mix_pre, w_in, mu, w0, w2, a0, a2, g2, k_k, k_a, r_k, lnx_w, lnx_b, lam_re, lam_im, log_step, b_re, b_im, c_re, c_im, d_skip, w_glu, b_glu, w_out, g_mix_post, g_ffn_pre, w_ff1, w_ff2, g_ffn_post, w_ple, g_ple_gate, w_ple_gate, g_ple_post):
    bsz, seq, d = x.shape
    width = w0.shape[-1]
    s5w = w_glu.shape[-1]
    dl, al, gl = w2.shape[1], a2.shape[1], g2.shape[1]
    dlp, alp, glp = _round_up(dl, LANES), _round_up(al, LANES), _round_up(gl, LANES)
    groups = s5w // S5_GROUP_CH
    nc = seq // S5_CHUNK
    m = bsz * seq
    o_wl = 3 * width
    o_al = o_wl + dl
    o_gl = o_al + al
    o_s5 = o_gl + gl

    def pad_cols(wm, to):
        return jnp.pad(wm, ((0, 0), (0, to - wm.shape[1])))

    def layer(h, lp):
        (p_i, g_mix_pre_i, wi, mu_i, w0_i, w2_i, a0_i, a2_i, g2_i, k_k_i, k_a_i, r_k_i, lnx_w_i, lnx_b_i,
         lam_re_i, lam_im_i, log_step_i, b_re_i, b_im_i, c_re_i, c_im_i, d_skip_i, w_glu_i, b_glu_i, w_out_i,
         g_mix_post_i, g_ffn_pre_i, w_ff1_i, w_ff2_i, g_ffn_post_i, w_ple_i, g_ple_gate_i, w_ple_gate_i,
         g_ple_post_i) = lp
        w_perm = _bf(jnp.concatenate([
            wi[:, :o_wl], wi[:, o_s5:],
            pad_cols(wi[:, o_wl:o_al], dlp), pad_cols(wi[:, o_al:o_gl], alp), pad_cols(wi[:, o_gl:o_s5], glp)],
            axis=1))
        mu_m = mu_i[:o_wl]
        mu_l = jnp.concatenate([jnp.pad(mu_i[o_wl:o_al], (0, dlp - dl)), jnp.pad(mu_i[o_al:o_gl], (0, alp - al)),
                                jnp.pad(mu_i[o_gl:o_s5], (0, glp - gl))])

        z = _mm_ares(h, g_mix_pre_i, w_perm, norm=True, epilogue="none", out_dtype=F32, name="in_proj")
        z3 = z.reshape(bsz, seq, z.shape[1])
        y_rwkv = _rwkv(z3, mu_m, mu_l, w0_i, _bf(_pad_rows(w2_i, dlp)), a0_i, _bf(_pad_rows(a2_i, alp)),
                       _bf(_pad_rows(g2_i, glp)), k_k_i, k_a_i, r_k_i, lnx_w_i, lnx_b_i,
                       width=width, s5w=s5w, name="rwkv")

        u = _bf(z3[:, :, o_wl:o_wl + s5w])
        u_g = jnp.transpose(u.reshape(bsz, nc, S5_CHUNK, groups, S5_GROUP_CH), (3, 0, 1, 2, 4))
        u_g = u_g.reshape(groups, bsz * nc, S5_CHUNK * S5_GROUP_CH)
        ops = _s5_operators(lam_re_i, lam_im_i, log_step_i, b_re_i, b_im_i, c_re_i, c_im_i, d_skip_i, nc=nc)
        y_g = _s5_core(u_g, *ops, nc=nc, name="s5")
        y_s = jnp.transpose(y_g.reshape(groups, bsz, nc, S5_CHUNK, S5_GROUP_CH), (1, 2, 3, 0, 4)).reshape(m, s5w)
        y_s5 = _mm_ares(y_s, jnp.ones((s5w,), F32), _bf(w_glu_i), norm=False, epilogue="glu", out_dtype=BF16,
                        extra=(y_s, b_glu_i), name="glu")

        mix_in = jnp.concatenate([y_rwkv.reshape(m, width), y_s5], axis=1)
        mixed = _mm_kt(mix_in, _bf(w_out_i), name="out_proj")
        h = _resid_norm(h, mixed, g_mix_post_i, name="mix_post")

        hid = _mm_ares(h, g_ffn_pre_i, _bf(w_ff1_i), norm=True, epilogue="relu2", out_dtype=BF16, name="ff1")
        f = _mm_kt(hid, _bf(w_ff2_i), name="ff2")
        h = _resid_norm(h, f, g_ffn_post_i, name="ffn_post")

        ge = _mm_ares(h, g_ple_gate_i, _bf(w_ple_gate_i), norm=True, epilogue="ple", out_dtype=F32,
                      extra=(_bf(p_i.reshape(m, -1)), _bf(w_ple_i)), name="ple")
        h = _resid_norm(h, ge, g_ple_post_i, name="ple_post")
        return h, None

    stacked = (p, g_mix_pre, w_in, mu, w0, w2, a0, a2, g2, k_k, k_a, r_k, lnx_w, lnx_b, lam_re, lam_im, log_step,
               b_re, b_im, c_re, c_im, d_skip, w_glu, b_glu, w_out, g_mix_post, g_ffn_pre, w_ff1, w_ff2, g_ffn_post,
               w_ple, g_ple_gate, w_ple_gate, g_ple_post)
    h, _ = lax.scan(layer, x.reshape(m, d), stacked)
    return h.reshape(bsz, seq, d)
```

```python
import functools
import math

import jax
import jax.numpy as jnp
from jax import lax
from jax.experimental import pallas as pl
from jax.experimental.pallas import tpu as pltpu

F32 = jnp.float32
BF16 = jnp.bfloat16

V7X_VMEM_BYTES = 64 * 1024 * 1024
LANES = 128
MXU_DIM = 256

VMEM_LIMIT = V7X_VMEM_BYTES - 8 * 1024 * 1024

RMS_EPS = 1e-6
LN_X_EPS = 64e-5
HEAD_DIM = 64
S5_GROUP_CH = 16
S5_CHUNK = 16
RWKV_CHUNK = 64
RWKV_PAIRS_PER_STEP = 4


def _bf(x):
    return x.astype(BF16)


def _dot(a, b):
    return jnp.dot(a, b, preferred_element_type=F32)


def _dot_nt(a, b):
    return lax.dot_general(a, b, (((1,), (1,)), ((), ())), preferred_element_type=F32)


def _dot_tn(a, b):
    return lax.dot_general(a, b, (((0,), (0,)), ((), ())), preferred_element_type=F32)


def _sigmoid(x):
    return 1.0 / (1.0 + jnp.exp(-x))


def _cparams(sem):
    return pltpu.CompilerParams(dimension_semantics=sem, vmem_limit_bytes=VMEM_LIMIT)


def _pick(n, cands):
    for c in cands:
        if n % c == 0:
            return c
    return n


def _rms(x, g):
    ms = jnp.mean(x * x, axis=-1, keepdims=True)
    return (x * lax.rsqrt(ms + RMS_EPS)) * g


def _mm_rows_kernel(*refs, epilogue):
    if epilogue == "glu":
        a_ref, w_ref, y_ref, b_ref, o_ref = refs
    elif epilogue == "ple":
        a_ref, w_ref, p_ref, wp_ref, o_ref = refs
    else:
        a_ref, w_ref, o_ref = refs
    acc = _dot(a_ref[...], w_ref[...])
    if epilogue == "none":
        o_ref[...] = acc.astype(o_ref.dtype)
    elif epilogue == "relu2":
        r = jnp.maximum(acc, 0.0)
        o_ref[...] = (r * r).astype(o_ref.dtype)
    elif epilogue == "glu":
        o_ref[...] = (y_ref[...].astype(F32) * _sigmoid(acc + b_ref[...])).astype(o_ref.dtype)
    elif epilogue == "ple":
        e = _dot(p_ref[...], wp_ref[...])
        o_ref[...] = (_sigmoid(acc) * e).astype(o_ref.dtype)


def _mm_rows(a, w, *, epilogue, out_dtype, extra=(), name):
    m, k = a.shape
    n = w.shape[1]
    tm = _pick(m, (1024, 512, 256, 128))
    tn = _pick(n, (1024, 512, 256, 128))
    in_specs = [
        pl.BlockSpec((tm, k), lambda i, j: (i, 0)),
        pl.BlockSpec((k, tn), lambda i, j: (0, j)),
    ]
    args = [a, w]
    if epilogue == "glu":
        y, b = extra
        in_specs += [pl.BlockSpec((tm, tn), lambda i, j: (i, j)),
                     pl.BlockSpec((1, tn), lambda i, j: (0, j))]
        args += [y, b.reshape(1, n).astype(F32)]
    elif epilogue == "ple":
        p, wp = extra
        kp = p.shape[1]
        in_specs += [pl.BlockSpec((tm, kp), lambda i, j: (i, 0)),
                     pl.BlockSpec((kp, tn), lambda i, j: (0, j))]
        args += [p, wp]
    return pl.pallas_call(
        functools.partial(_mm_rows_kernel, epilogue=epilogue),
        grid=(m // tm, n // tn),
        in_specs=in_specs,
        out_specs=pl.BlockSpec((tm, tn), lambda i, j: (i, j)),
        out_shape=jax.ShapeDtypeStruct((m, n), out_dtype),
        compiler_params=_cparams(("parallel", "arbitrary")),
        name=name,
    )(*args)


def _mm_kt_kernel(a_ref, w_ref, o_ref, *, slab):
    n = o_ref.shape[1]
    a = a_ref[...]

    @pl.when(pl.program_id(1) == 0)
    def _():
        for s in range(n // slab):
            sl = slice(s * slab, (s + 1) * slab)
            o_ref[:, sl] = _dot(a, w_ref[:, sl])

    @pl.when(pl.program_id(1) > 0)
    def _():
        for s in range(n // slab):
            sl = slice(s * slab, (s + 1) * slab)
            o_ref[:, sl] += _dot(a, w_ref[:, sl])


def _mm_kt(a, w, *, name):
    m, k = a.shape
    n = w.shape[1]
    tm = _pick(m, (1024, 512, 256, 128))
    tk = _pick(k, (512, 256, 128))
    slab = _pick(n, (512, 256, 128))
    return pl.pallas_call(
        functools.partial(_mm_kt_kernel, slab=slab),
        grid=(m // tm, k // tk),
        in_specs=[pl.BlockSpec((tm, tk), lambda i, kk: (i, kk)),
                  pl.BlockSpec((tk, n), lambda i, kk: (kk, 0))],
        out_specs=pl.BlockSpec((tm, n), lambda i, kk: (i, 0)),
        out_shape=jax.ShapeDtypeStruct((m, n), F32),
        compiler_params=_cparams(("parallel", "arbitrary")),
        name=name,
    )(a, w)


def _norm_cast_kernel(x_ref, g_ref, o_ref):
    o_ref[...] = _bf(_rms(x_ref[...], g_ref[...]))


def _norm_cast(x, g, *, name):
    m, d = x.shape
    tm = _pick(m, (256, 128))
    spec = pl.BlockSpec((tm, d), lambda i: (i, 0))
    return pl.pallas_call(
        _norm_cast_kernel,
        grid=(m // tm,),
        in_specs=[spec, pl.BlockSpec((1, d), lambda i: (0, 0))],
        out_specs=spec,
        out_shape=jax.ShapeDtypeStruct((m, d), BF16),
        compiler_params=_cparams(("parallel",)),
        name=name,
    )(x, g.reshape(1, d).astype(F32))


def _resid_norm_kernel(h_ref, f_ref, gp_ref, gn_ref, o_ref, on_ref):
    h = h_ref[...] + _rms(f_ref[...], gp_ref[...])
    o_ref[...] = h
    on_ref[...] = _bf(_rms(h, gn_ref[...]))


def _resid_norm(h, f, g_post, g_next, *, name):
    m, d = h.shape
    tm = _pick(m, (256, 128))
    spec = pl.BlockSpec((tm, d), lambda i: (i, 0))
    vec = pl.BlockSpec((1, d), lambda i: (0, 0))
    return pl.pallas_call(
        _resid_norm_kernel,
        grid=(m // tm,),
        in_specs=[spec, spec, vec, vec],
        out_specs=[spec, spec],
        out_shape=[jax.ShapeDtypeStruct((m, d), F32), jax.ShapeDtypeStruct((m, d), BF16)],
        compiler_params=_cparams(("parallel",)),
        name=name,
    )(h, f, g_post.reshape(1, d).astype(F32), g_next.reshape(1, d).astype(F32))


def _segsum(x, ones_bd):
    hi = _bf(x)
    lo = _bf(x - hi.astype(F32))
    outs = []
    for s in range(x.shape[1] // MXU_DIM):
        sl = slice(s * MXU_DIM, (s + 1) * MXU_DIM)
        outs.append(_dot(hi[:, sl], ones_bd) + _dot(lo[:, sl], ones_bd))
    return jnp.concatenate(outs, axis=1)


def _swap_pair_halves(x, first_half):
    w = x.shape[1]
    return jnp.where(first_half, pltpu.roll(x, w - HEAD_DIM, axis=1), pltpu.roll(x, HEAD_DIM, axis=1))


def _rwkv_kernel(zm_ref, zl_ref, mum_ref, mul_ref, w0_ref, w2_ref, a0_ref, a2_ref, g2_ref,
                 kk_ref, ka_ref, rk_ref, lw_ref, lb_ref,
                 o_ref,
                 st_ref, carm_ref, carl_ref,
                 a_s, r_s, b_s, k_s, bh_s, kh_s, vsw_s, pe_s, y_s, y0_s,
                 *, tt, width, dlp, alp):
    c_len = RWKV_CHUNK
    n = HEAD_DIM
    pairs = width // LANES
    pp = RWKV_PAIRS_PER_STEP if pairs % RWKV_PAIRS_PER_STEP == 0 else 1

    @pl.when(pl.program_id(1) == 0)
    def _():
        st_ref[...] = jnp.zeros_like(st_ref)
        carm_ref[...] = jnp.zeros_like(carm_ref)
        carl_ref[...] = jnp.zeros_like(carl_ref)

    row = lax.broadcasted_iota(jnp.int32, (tt, 1), 0)

    def shift_lerp(x, car_ref, mu):
        xs = pltpu.roll(x, 1, axis=0)
        xs = jnp.where(row == 0, car_ref[...], xs)
        car_ref[...] = x[tt - 1:tt, :]
        return x + (xs - x) * mu

    zm = shift_lerp(zm_ref[...], carm_ref, mum_ref[...])
    zl = shift_lerp(zl_ref[...], carl_ref, mul_ref[...])
    r = zm[:, :width]
    k = zm[:, width:2 * width]
    v = zm[:, 2 * width:]
    xw = zl[:, :dlp]
    xa = zl[:, dlp:dlp + alp]
    xg = zl[:, dlp + alp:]

    wr = w0_ref[...] + _dot(_bf(jnp.tanh(xw)), w2_ref[...])
    softplus_neg = jnp.maximum(-wr, 0.0) + jnp.log(1.0 + jnp.exp(-jnp.abs(wr)))
    wlog = -jnp.exp(-softplus_neg - 0.5)
    a = _sigmoid(a0_ref[...] + _dot(_bf(xa), a2_ref[...]))
    g = _dot(_bf(_sigmoid(xg)), g2_ref[...])

    ri = lax.broadcasted_iota(jnp.int32, (MXU_DIM, MXU_DIM), 0)
    ci = lax.broadcasted_iota(jnp.int32, (MXU_DIM, MXU_DIM), 1)
    shift = int(math.log2(n))
    ones_bd = jnp.where((ri >> shift) == (ci >> shift), 1.0, 0.0).astype(BF16)

    kkv = k * kk_ref[...]
    ssq = _segsum(kkv * kkv, ones_bd)
    kkn = kkv / jnp.maximum(jnp.sqrt(ssq), 1e-12)
    k2 = k * (1.0 + (a - 1.0) * ka_ref[...])
    beta = kkn * a
    bonus = _segsum(r * k2 * rk_ref[...], ones_bd) * v

    rt_i = lax.broadcasted_iota(jnp.int32, (tt, tt), 0)
    ct_i = lax.broadcasted_iota(jnp.int32, (tt, tt), 1)
    cshift = int(math.log2(c_len))
    same = (rt_i >> cshift) == (ct_i >> cshift)
    lt_bd = jnp.where(same & (ct_i <= rt_i), 1.0, 0.0).astype(BF16)
    le_bd = jnp.where(same, 1.0, 0.0).astype(BF16)
    w_hi = _bf(wlog)
    w_lo = _bf(wlog - w_hi.astype(F32))
    cum = _dot(lt_bd, w_hi) + _dot(lt_bd, w_lo)
    cend = _dot(le_bd, w_hi) + _dot(le_bd, w_lo)

    first_half = (lax.broadcasted_iota(jnp.int32, (1, width), 1) & (LANES - 1)) < n
    inv = jnp.exp(-cum)
    dend = jnp.exp(cend - cum)
    a_s[...] = _bf(-kkn * jnp.exp(cum - wlog))
    r_s[...] = _bf(r * jnp.exp(cum))
    b_s[...] = _bf(beta * inv)
    k_s[...] = _bf(k2 * inv)
    bh_s[...] = _bf(beta * dend)
    kh_s[...] = _bf(k2 * dend)
    vsw_s[...] = _bf(_swap_pair_halves(v, first_half))
    pe_s[...] = jnp.exp(cend)

    lane = lax.broadcasted_iota(jnp.int32, (c_len, LANES), 1)
    rowi = lax.broadcasted_iota(jnp.int32, (c_len, LANES), 0)
    m0f = lane < n
    m0b = jnp.where(m0f, 1.0, 0.0).astype(BF16)
    m1b = jnp.where(m0f, 0.0, 1.0).astype(BF16)
    sidx = lane & (n - 1)
    strict2 = sidx < rowi
    incl2 = sidx <= rowi
    r2 = lax.broadcasted_iota(jnp.int32, (LANES, LANES), 0)
    l2 = lax.broadcasted_iota(jnp.int32, (LANES, LANES), 1)
    bd = (r2 < n) == (l2 < n)
    eye2 = r2 == l2
    zer = jnp.zeros((c_len, LANES), BF16)
    n_mid_rounds = int(math.log2(c_len)) - 2

    def group_body(gidx, carry):
        for pi in range(pp):
            p = gidx * pp + pi
            lanes = pl.ds(pl.multiple_of(p * LANES, LANES), LANES)
            s_bd = st_ref[p]
            for c in range(tt // c_len):
                rows = slice(c * c_len, (c + 1) * c_len)
                a_p = a_s[rows, lanes]
                r_p = r_s[rows, lanes]
                bh_p = bh_s[rows, lanes]
                kh_p = kh_s[rows, lanes]
                vsw_p = vsw_s[rows, lanes]
                pe_p = pe_s[c * c_len:c * c_len + 1, lanes]
                bk = jnp.concatenate([b_s[rows, lanes], k_s[rows, lanes]], axis=0)
                ws = []
                mn = None
                for q in range(LANES // n):
                    mq, mo = (m0b, m1b) if q == 0 else (m1b, m0b)
                    aq = a_p * mq
                    sc = _dot_nt(jnp.concatenate([aq, r_p * mq], axis=0), bk)
                    top = _bf(jnp.where(strict2, sc[:c_len], 0.0))
                    bot = _bf(jnp.where(incl2, sc[c_len:], 0.0))
                    vq = vsw_p * mo
                    z = aq.astype(F32) + _dot(top, jnp.concatenate([zer, vq], axis=0))
                    t = _dot(top[:, :n], jnp.concatenate([_bf(z), top], axis=1))
                    z = z + t[:, :LANES]
                    pw = t[:, LANES:]
                    for _ in range(n_mid_rounds):
                        pb = _bf(pw)
                        t = _dot(pb[:, :n], jnp.concatenate([_bf(z), pb], axis=1))
                        z = z + t[:, :LANES]
                        pw = t[:, LANES:]
                    z = z + _dot(_bf(pw)[:, :n], _bf(z))
                    qv = jnp.concatenate([_bf(z), vq], axis=0)
                    ws.append(_dot(bot, qv))
                    gq = _dot_tn(qv, jnp.concatenate([bh_p * mq, kh_p * mq], axis=0))
                    mn = gq if mn is None else mn + gq
                rhat = r_p.astype(F32) + jnp.where(m0f, ws[0], ws[1])
                y0_s[rows, lanes] = jnp.where(m0f, ws[1], ws[0])
                m_bd = jnp.where(bd, mn, 0.0) + jnp.where(eye2, pe_p, 0.0)
                n_sw = jnp.where(bd, 0.0, mn)
                n_bd = jnp.concatenate([n_sw[n:], n_sw[:n]], axis=0)
                sb = _bf(s_bd)
                y_s[rows, lanes] = _dot_nt(_bf(rhat), sb)
                s_bd = _dot(sb, _bf(m_bd)) + n_bd
            st_ref[p] = s_bd
        return carry

    lax.fori_loop(0, pairs // pp, group_body, 0)

    y = y_s[...] + _swap_pair_halves(y0_s[...], first_half)
    mean = _segsum(y, ones_bd) * (1.0 / n)
    yc = y - mean
    var = _segsum(yc * yc, ones_bd) * (1.0 / n)
    yn = yc * lax.rsqrt(var + LN_X_EPS) * lw_ref[...] + lb_ref[...]
    o_ref[...] = ((yn + bonus) * g).astype(o_ref.dtype)


def _rwkv(z, mu_m, mu_l, w0, w2p, a0, a2p, g2p, k_k, k_a, r_k, lnx_w, lnx_b, *, width, s5w, name):
    b, t, zc = z.shape
    dlp, alp, glp = w2p.shape[0], a2p.shape[0], g2p.shape[0]
    lw = dlp + alp + glp
    tt = _pick(t, (128, 64))
    lora_blk = (3 * width + s5w) // lw
    assert lora_blk * lw == 3 * width + s5w
    pairs = width // LANES

    def row(x):
        return x.reshape(1, -1).astype(F32)

    vec = pl.BlockSpec((1, width), lambda bi, i: (0, 0))
    kern = functools.partial(_rwkv_kernel, tt=tt, width=width, dlp=dlp, alp=alp)
    return pl.pallas_call(
        kern,
        grid=(b, t // tt),
        in_specs=[
            pl.BlockSpec((None, tt, 3 * width), lambda bi, i: (bi, i, 0)),
            pl.BlockSpec((None, tt, lw), lambda bi, i: (bi, i, lora_blk)),
            pl.BlockSpec((1, 3 * width), lambda bi, i: (0, 0)),
            pl.BlockSpec((1, lw), lambda bi, i: (0, 0)),
            vec,
            pl.BlockSpec((dlp, width), lambda bi, i: (0, 0)),
            vec,
            pl.BlockSpec((alp, width), lambda bi, i: (0, 0)),
            pl.BlockSpec((glp, width), lambda bi, i: (0, 0)),
            vec, vec, vec, vec, vec,
        ],
        out_specs=pl.BlockSpec((None, tt, width), lambda bi, i: (bi, i, 0)),
        out_shape=jax.ShapeDtypeStruct((b, t, width), BF16),
        scratch_shapes=[
            pltpu.VMEM((pairs, LANES, LANES), F32),
            pltpu.VMEM((1, 3 * width), F32),
            pltpu.VMEM((1, lw), F32),
        ] + [pltpu.VMEM((tt, width), BF16)] * 7 + [pltpu.VMEM((tt, width), F32)] * 3,
        compiler_params=_cparams(("parallel", "arbitrary")),
        name=name,
    )(z, z, row(mu_m), row(mu_l), row(w0), w2p, row(a0), a2p, g2p,
      row(k_k), row(k_a), row(r_k), row(lnx_w), row(lnx_b))


def _gelu_tanh(x):
    return x * (0.5 * (1.0 + jnp.tanh(0.7978845608028654 * (x + 0.044715 * (x * x * x)))))


def _s5_kernel(u_ref, t_ref, w_ref, g_ref, m1_ref, m2_ref, o_ref, *, gb, nc, nlev):
    rows = u_ref.shape[1]
    cidx = lax.broadcasted_iota(jnp.int32, (rows, 1), 0) & (nc - 1)
    for gi in range(gb):
        u = u_ref[gi]
        y_intra = _dot(u, t_ref[gi])
        x = _dot(u, w_ref[gi])
        half = x.shape[1] // 2
        for lev in range(nlev):
            sh = 1 << lev
            xs = jnp.where(cidx >= sh, pltpu.roll(x, sh, axis=0), 0.0)
            xsw = pltpu.roll(xs, half, axis=1)
            x = x + m1_ref[gi, lev:lev + 1, :] * xs + m2_ref[gi, lev:lev + 1, :] * xsw
        xp = jnp.where(cidx >= 1, pltpu.roll(x, 1, axis=0), 0.0)
        y = y_intra + _dot(_bf(xp), g_ref[gi])
        o_ref[gi] = _gelu_tanh(y).astype(o_ref.dtype)


def _s5_core(u_g, t_m, w_m, g_m, m1, m2, *, nc, name):
    groups, rows, lc = u_g.shape
    sp = w_m.shape[2]
    nlev = m1.shape[1]
    gb = _pick(groups, (8, 4, 2, 1))

    def spec(shape):
        return pl.BlockSpec((gb,) + shape, lambda i: (i, 0, 0))

    return pl.pallas_call(
        functools.partial(_s5_kernel, gb=gb, nc=nc, nlev=nlev),
        grid=(groups // gb,),
        in_specs=[spec((rows, lc)), spec((lc, lc)), spec((lc, sp)), spec((sp, lc)),
                  spec((nlev, sp)), spec((nlev, sp))],
        out_specs=spec((rows, lc)),
        out_shape=jax.ShapeDtypeStruct((groups, rows, lc), BF16),
        compiler_params=_cparams(("parallel",)),
        name=name,
    )(u_g, t_m, w_m, g_m, m1, m2)


def _s5_operators(lam_re, lam_im, log_step, b_re, b_im, c_re, c_im, d_skip, *, nc):
    groups, sp = lam_re.shape
    ch = b_re.shape[2]
    lc = S5_CHUNK
    lr = lam_re.astype(F32)
    li = lam_im.astype(F32)
    dt = jnp.exp(log_step.astype(F32))[:, None]

    def apow(nsteps):
        nsteps = jnp.asarray(nsteps, F32)[:, None, None]
        mag = jnp.exp(lr * dt * nsteps)
        ang = li * dt * nsteps
        return mag * jnp.cos(ang), mag * jnp.sin(ang)

    ar, ai = apow(jnp.arange(lc + 1))
    den = lr * lr + li * li
    q_re = ((ar[1] - 1.0) * lr + ai[1] * li) / den
    q_im = (ai[1] * lr - (ar[1] - 1.0) * li) / den
    bb_re = q_re[:, :, None] * b_re - q_im[:, :, None] * b_im
    bb_im = q_re[:, :, None] * b_im + q_im[:, :, None] * b_re
    ce_re = c_re[None] * ar[:, :, None, :] - c_im[None] * ai[:, :, None, :]
    ce_im = c_re[None] * ai[:, :, None, :] + c_im[None] * ar[:, :, None, :]
    hp = lax.Precision.HIGHEST
    taps = (jnp.einsum('tghp,gpk->tghk', ce_re[:lc], bb_re, precision=hp)
            - jnp.einsum('tghp,gpk->tghk', ce_im[:lc], bb_im, precision=hp))
    taps = jnp.concatenate([taps, jnp.zeros_like(taps[:1])], axis=0)
    s_i = jnp.arange(lc)[:, None]
    t_i = jnp.arange(lc)[None, :]
    lag = jnp.where(t_i >= s_i, t_i - s_i, lc)
    t5 = taps[lag]
    t_m = jnp.transpose(t5, (2, 0, 4, 1, 3))
    skip = (jnp.eye(lc, dtype=F32)[None, :, None, :, None]
            * jnp.eye(ch, dtype=F32)[None, None, :, None, :]
            * d_skip.reshape(groups, 1, 1, 1, ch).astype(F32))
    t_m = (t_m + skip).reshape(groups, lc * ch, lc * ch)
    arr = ar[lc - 1 - jnp.arange(lc)]
    air = ai[lc - 1 - jnp.arange(lc)]
    w_re = arr[:, :, :, None] * bb_re[None] - air[:, :, :, None] * bb_im[None]
    w_im = arr[:, :, :, None] * bb_im[None] + air[:, :, :, None] * bb_re[None]
    w_m = jnp.concatenate([w_re, w_im], axis=2)
    w_m = jnp.transpose(w_m, (1, 0, 3, 2)).reshape(groups, lc * ch, 2 * sp)
    g_re = jnp.transpose(ce_re[1:], (1, 3, 0, 2))
    g_im = jnp.transpose(ce_im[1:], (1, 3, 0, 2))
    g_m = jnp.concatenate([g_re, -g_im], axis=1).reshape(groups, 2 * sp, lc * ch)
    nlev = max(1, int(math.ceil(math.log2(nc))))
    mr, mi = apow(lc * (2.0 ** jnp.arange(nlev)))
    m1 = jnp.transpose(jnp.concatenate([mr, mr], axis=2), (1, 0, 2))
    m2 = jnp.transpose(jnp.concatenate([-mi, mi], axis=2), (1, 0, 2))
    return _bf(t_m), _bf(w_m), _bf(g_m), m1, m2


def _pad_rows(w, to):
    return jnp.pad(w, ((0, to - w.shape[0]), (0, 0)))


def _round_up(x, m):
    return (x + m - 1) // m * m


def kernel(x, p, g_mix_pre, w_in, mu, w0, w2, a0, a2, g2, k_k, k_a, r_k, lnx_w, lnx_b, lam_re, lam_im, log_step, b_re, b_im, c_re, c_im, d_skip, w_glu, b_glu, w_out, g_mix_post, g_ffn_pre, w_ff1, w_ff2, g_ffn_post, w_ple, g_ple_gate, w_ple_gate, g_ple_post):
    bsz, seq, d = x.shape
    width = w0.shape[-1]
    s5w = w_glu.shape[-1]
    dl, al, gl = w2.shape[1], a2.shape[1], g2.shape[1]
    dlp, alp, glp = _round_up(dl, LANES), _round_up(al, LANES), _round_up(gl, LANES)
    groups = s5w // S5_GROUP_CH
    nc = seq // S5_CHUNK
    m = bsz * seq
    o_wl = 3 * width
    o_al = o_wl + dl
    o_gl = o_al + al
    o_s5 = o_gl + gl

    def pad_cols(wm, to):
        return jnp.pad(wm, ((0, 0), (0, to - wm.shape[1])))

    def layer(carry, lp):
        h, hn = carry
        (p_i, g_next_i, wi, mu_i, w0_i, w2_i, a0_i, a2_i, g2_i, k_k_i, k_a_i, r_k_i, lnx_w_i, lnx_b_i,
         lam_re_i, lam_im_i, log_step_i, b_re_i, b_im_i, c_re_i, c_im_i, d_skip_i, w_glu_i, b_glu_i, w_out_i,
         g_mix_post_i, g_ffn_pre_i, w_ff1_i, w_ff2_i, g_ffn_post_i, w_ple_i, g_ple_gate_i, w_ple_gate_i,
         g_ple_post_i) = lp
        w_perm = _bf(jnp.concatenate([
            wi[:, :o_wl], wi[:, o_s5:],
            pad_cols(wi[:, o_wl:o_al], dlp), pad_cols(wi[:, o_al:o_gl], alp), pad_cols(wi[:, o_gl:o_s5], glp)],
            axis=1))
        mu_m = mu_i[:o_wl]
        mu_l = jnp.concatenate([jnp.pad(mu_i[o_wl:o_al], (0, dlp - dl)), jnp.pad(mu_i[o_al:o_gl], (0, alp - al)),
                                jnp.pad(mu_i[o_gl:o_s5], (0, glp - gl))])

        z = _mm_rows(hn, w_perm, epilogue="none", out_dtype=F32, name="in_proj")
        z3 = z.reshape(bsz, seq, z.shape[1])
        y_rwkv = _rwkv(z3, mu_m, mu_l, w0_i, _bf(_pad_rows(w2_i, dlp)), a0_i, _bf(_pad_rows(a2_i, alp)),
                       _bf(_pad_rows(g2_i, glp)), k_k_i, k_a_i, r_k_i, lnx_w_i, lnx_b_i,
                       width=width, s5w=s5w, name="rwkv")

        u = _bf(z3[:, :, o_wl:o_wl + s5w])
        u_g = jnp.transpose(u.reshape(bsz, nc, S5_CHUNK, groups, S5_GROUP_CH), (3, 0, 1, 2, 4))
        u_g = u_g.reshape(groups, bsz * nc, S5_CHUNK * S5_GROUP_CH)
        ops = _s5_operators(lam_re_i, lam_im_i, log_step_i, b_re_i, b_im_i, c_re_i, c_im_i, d_skip_i, nc=nc)
        y_g = _s5_core(u_g, *ops, nc=nc, name="s5")
        y_s = jnp.transpose(y_g.reshape(groups, bsz, nc, S5_CHUNK, S5_GROUP_CH), (1, 2, 3, 0, 4)).reshape(m, s5w)
        y_s5 = _mm_rows(y_s, _bf(w_glu_i), epilogue="glu", out_dtype=BF16, extra=(y_s, b_glu_i), name="glu")

        mix_in = jnp.concatenate([y_rwkv.reshape(m, width), y_s5], axis=1)
        mixed = _mm_kt(mix_in, _bf(w_out_i), name="out_proj")
        h, hn = _resid_norm(h, mixed, g_mix_post_i, g_ffn_pre_i, name="mix_post")

        hid = _mm_rows(hn, _bf(w_ff1_i), epilogue="relu2", out_dtype=BF16, name="ff1")
        f = _mm_kt(hid, _bf(w_ff2_i), name="ff2")
        h, hn = _resid_norm(h, f, g_ffn_post_i, g_ple_gate_i, name="ffn_post")

        ge = _mm_rows(hn, _bf(w_ple_gate_i), epilogue="ple", out_dtype=F32,
                      extra=(_bf(p_i.reshape(m, -1)), _bf(w_ple_i)), name="ple")
        h, hn = _resid_norm(h, ge, g_ple_post_i, g_next_i, name="ple_post")
        return (h, hn), None

    g_next = jnp.roll(g_mix_pre, -1, axis=0)
    stacked = (p, g_next, w_in, mu, w0, w2, a0, a2, g2, k_k, k_a, r_k, lnx_w, lnx_b, lam_re, lam_im, log_step,
               b_re, b_im, c_re, c_im, d_skip, w_glu, b_glu, w_out, g_mix_post, g_ffn_pre, w_ff1, w_ff2, g_ffn_post,
               w_ple, g_ple_gate, w_ple_gate, g_ple_post)
    h0 = x.reshape(m, d)
    hn0 = _norm_cast(h0, g_mix_pre[0], name="pre_norm")
    (h, _), _ = lax.scan(layer, (h0, hn0), stacked)
    return h.reshape(bsz, seq, d)
```

```python
import functools
import math

import jax
import jax.numpy as jnp
from jax import lax
from jax.experimental import pallas as pl
from jax.experimental.pallas import tpu as pltpu

F32 = jnp.float32
BF16 = jnp.bfloat16

V7X_VMEM_BYTES = 64 * 1024 * 1024
LANES = 128
MXU_DIM = 256

VMEM_LIMIT = V7X_VMEM_BYTES - 8 * 1024 * 1024

RMS_EPS = 1e-6
LN_X_EPS = 64e-5
HEAD_DIM = 64
S5_GROUP_CH = 16
S5_CHUNK = 16
RWKV_CHUNK = 64
RWKV_PAIRS_PER_STEP = 4


def _bf(x):
    return x.astype(BF16)


def _dot(a, b):
    return jnp.dot(a, b, preferred_element_type=F32)


def _dot_nt(a, b):
    return lax.dot_general(a, b, (((1,), (1,)), ((), ())), preferred_element_type=F32)


def _dot_tn(a, b):
    return lax.dot_general(a, b, (((0,), (0,)), ((), ())), preferred_element_type=F32)


def _sigmoid(x):
    return 1.0 / (1.0 + jnp.exp(-x))


def _cparams(sem):
    return pltpu.CompilerParams(dimension_semantics=sem, vmem_limit_bytes=VMEM_LIMIT)


def _pick(n, cands):
    for c in cands:
        if n % c == 0:
            return c
    return n


def _rms(x, g):
    ms = jnp.mean(x * x, axis=-1, keepdims=True)
    return (x * lax.rsqrt(ms + RMS_EPS)) * g


def _mm_rows_kernel(*refs, epilogue):
    if epilogue == "glu":
        a_ref, w_ref, y_ref, b_ref, o_ref = refs
    elif epilogue == "ple":
        a_ref, w_ref, p_ref, wp_ref, o_ref = refs
    else:
        a_ref, w_ref, o_ref = refs
    acc = _dot(a_ref[...], w_ref[...])
    if epilogue == "none":
        o_ref[...] = acc.astype(o_ref.dtype)
    elif epilogue == "relu2":
        r = jnp.maximum(acc, 0.0)
        o_ref[...] = (r * r).astype(o_ref.dtype)
    elif epilogue == "glu":
        o_ref[...] = (y_ref[...].astype(F32) * _sigmoid(acc + b_ref[...])).astype(o_ref.dtype)
    elif epilogue == "ple":
        e = _dot(p_ref[...], wp_ref[...])
        o_ref[...] = (_sigmoid(acc) * e).astype(o_ref.dtype)


def _mm_rows(a, w, *, epilogue, out_dtype, extra=(), name):
    m, k = a.shape
    n = w.shape[1]
    tm = _pick(m, (1024, 512, 256, 128))
    tn = _pick(n, (1024, 512, 256, 128))
    in_specs = [
        pl.BlockSpec((tm, k), lambda i, j: (i, 0)),
        pl.BlockSpec((k, tn), lambda i, j: (0, j)),
    ]
    args = [a, w]
    if epilogue == "glu":
        y, b = extra
        in_specs += [pl.BlockSpec((tm, tn), lambda i, j: (i, j)),
                     pl.BlockSpec((1, tn), lambda i, j: (0, j))]
        args += [y, b.reshape(1, n).astype(F32)]
    elif epilogue == "ple":
        p, wp = extra
        kp = p.shape[1]
        in_specs += [pl.BlockSpec((tm, kp), lambda i, j: (i, 0)),
                     pl.BlockSpec((kp, tn), lambda i, j: (0, j))]
        args += [p, wp]
    return pl.pallas_call(
        functools.partial(_mm_rows_kernel, epilogue=epilogue),
        grid=(m // tm, n // tn),
        in_specs=in_specs,
        out_specs=pl.BlockSpec((tm, tn), lambda i, j: (i, j)),
        out_shape=jax.ShapeDtypeStruct((m, n), out_dtype),
        compiler_params=_cparams(("parallel", "arbitrary")),
        name=name,
    )(*args)


def _mm_kt_kernel(a_ref, w_ref, o_ref, *, slab):
    n = o_ref.shape[1]
    a = a_ref[...]

    @pl.when(pl.program_id(1) == 0)
    def _():
        for s in range(n // slab):
            sl = slice(s * slab, (s + 1) * slab)
            o_ref[:, sl] = _dot(a, w_ref[:, sl])

    @pl.when(pl.program_id(1) > 0)
    def _():
        for s in range(n // slab):
            sl = slice(s * slab, (s + 1) * slab)
            o_ref[:, sl] += _dot(a, w_ref[:, sl])


def _mm_kt(a, w, *, name):
    m, k = a.shape
    n = w.shape[1]
    tm = _pick(m, (1024, 512, 256, 128))
    tk = _pick(k, (512, 256, 128))
    slab = _pick(n, (512, 256, 128))
    return pl.pallas_call(
        functools.partial(_mm_kt_kernel, slab=slab),
        grid=(m // tm, k // tk),
        in_specs=[pl.BlockSpec((tm, tk), lambda i, kk: (i, kk)),
                  pl.BlockSpec((tk, n), lambda i, kk: (kk, 0))],
        out_specs=pl.BlockSpec((tm, n), lambda i, kk: (i, 0)),
        out_shape=jax.ShapeDtypeStruct((m, n), F32),
        compiler_params=_cparams(("parallel", "arbitrary")),
        name=name,
    )(a, w)


def _norm_cast_kernel(x_ref, g_ref, o_ref):
    o_ref[...] = _bf(_rms(x_ref[...], g_ref[...]))


def _norm_cast(x, g, *, name):
    m, d = x.shape
    tm = _pick(m, (256, 128))
    spec = pl.BlockSpec((tm, d), lambda i: (i, 0))
    return pl.pallas_call(
        _norm_cast_kernel,
        grid=(m // tm,),
        in_specs=[spec, pl.BlockSpec((1, d), lambda i: (0, 0))],
        out_specs=spec,
        out_shape=jax.ShapeDtypeStruct((m, d), BF16),
        compiler_params=_cparams(("parallel",)),
        name=name,
    )(x, g.reshape(1, d).astype(F32))


def _resid_norm_kernel(h_ref, f_ref, gp_ref, gn_ref, o_ref, on_ref):
    h = h_ref[...] + _rms(f_ref[...], gp_ref[...])
    o_ref[...] = h
    on_ref[...] = _bf(_rms(h, gn_ref[...]))


def _resid_norm(h, f, g_post, g_next, *, name):
    m, d = h.shape
    tm = _pick(m, (256, 128))
    spec = pl.BlockSpec((tm, d), lambda i: (i, 0))
    vec = pl.BlockSpec((1, d), lambda i: (0, 0))
    return pl.pallas_call(
        _resid_norm_kernel,
        grid=(m // tm,),
        in_specs=[spec, spec, vec, vec],
        out_specs=[spec, spec],
        out_shape=[jax.ShapeDtypeStruct((m, d), F32), jax.ShapeDtypeStruct((m, d), BF16)],
        compiler_params=_cparams(("parallel",)),
        name=name,
    )(h, f, g_post.reshape(1, d).astype(F32), g_next.reshape(1, d).astype(F32))


def _segsum(x, ones_bd):
    hi = _bf(x)
    lo = _bf(x - hi.astype(F32))
    outs = []
    for s in range(x.shape[1] // MXU_DIM):
        sl = slice(s * MXU_DIM, (s + 1) * MXU_DIM)
        outs.append(_dot(hi[:, sl], ones_bd) + _dot(lo[:, sl], ones_bd))
    return jnp.concatenate(outs, axis=1)


def _swap_pair_halves(x, first_half):
    w = x.shape[1]
    return jnp.where(first_half, pltpu.roll(x, w - HEAD_DIM, axis=1), pltpu.roll(x, HEAD_DIM, axis=1))


def _rwkv_kernel(zm_ref, zl_ref, mum_ref, mul_ref, w0_ref, w2_ref, a0_ref, a2_ref, g2_ref,
                 kk_ref, ka_ref, rk_ref, lw_ref, lb_ref,
                 o_ref,
                 st_ref, carm_ref, carl_ref,
                 a_s, r_s, b_s, k_s, bh_s, kh_s, vsw_s, pe_s, y_s, y0_s,
                 *, tt, width, dlp, alp):
    c_len = RWKV_CHUNK
    n = HEAD_DIM
    pairs = width // LANES
    pp = RWKV_PAIRS_PER_STEP if pairs % RWKV_PAIRS_PER_STEP == 0 else 1

    @pl.when(pl.program_id(1) == 0)
    def _():
        st_ref[...] = jnp.zeros_like(st_ref)
        carm_ref[...] = jnp.zeros_like(carm_ref)
        carl_ref[...] = jnp.zeros_like(carl_ref)

    row = lax.broadcasted_iota(jnp.int32, (tt, 1), 0)

    def shift_lerp(x, car_ref, mu):
        xs = pltpu.roll(x, 1, axis=0)
        xs = jnp.where(row == 0, car_ref[...], xs)
        car_ref[...] = x[tt - 1:tt, :]
        return x + (xs - x) * mu

    zm = shift_lerp(zm_ref[...], carm_ref, mum_ref[...])
    zl = shift_lerp(zl_ref[...], carl_ref, mul_ref[...])
    r = zm[:, :width]
    k = zm[:, width:2 * width]
    v = zm[:, 2 * width:]
    xw = zl[:, :dlp]
    xa = zl[:, dlp:dlp + alp]
    xg = zl[:, dlp + alp:]

    wr = w0_ref[...] + _dot(_bf(jnp.tanh(xw)), w2_ref[...])
    softplus_neg = jnp.maximum(-wr, 0.0) + jnp.log(1.0 + jnp.exp(-jnp.abs(wr)))
    wlog = -jnp.exp(-softplus_neg - 0.5)
    a = _sigmoid(a0_ref[...] + _dot(_bf(xa), a2_ref[...]))
    g = _dot(_bf(_sigmoid(xg)), g2_ref[...])

    ri = lax.broadcasted_iota(jnp.int32, (MXU_DIM, MXU_DIM), 0)
    ci = lax.broadcasted_iota(jnp.int32, (MXU_DIM, MXU_DIM), 1)
    shift = int(math.log2(n))
    ones_bd = jnp.where((ri >> shift) == (ci >> shift), 1.0, 0.0).astype(BF16)

    kkv = k * kk_ref[...]
    ssq = _segsum(kkv * kkv, ones_bd)
    kkn = kkv / jnp.maximum(jnp.sqrt(ssq), 1e-12)
    k2 = k * (1.0 + (a - 1.0) * ka_ref[...])
    beta = kkn * a
    bonus = _segsum(r * k2 * rk_ref[...], ones_bd) * v

    rt_i = lax.broadcasted_iota(jnp.int32, (tt, tt), 0)
    ct_i = lax.broadcasted_iota(jnp.int32, (tt, tt), 1)
    cshift = int(math.log2(c_len))
    same = (rt_i >> cshift) == (ct_i >> cshift)
    lt_bd = jnp.where(same & (ct_i <= rt_i), 1.0, 0.0).astype(BF16)
    le_bd = jnp.where(same, 1.0, 0.0).astype(BF16)
    w_hi = _bf(wlog)
    w_lo = _bf(wlog - w_hi.astype(F32))
    cum = _dot(lt_bd, w_hi) + _dot(lt_bd, w_lo)
    cend = _dot(le_bd, w_hi) + _dot(le_bd, w_lo)

    first_half = (lax.broadcasted_iota(jnp.int32, (1, width), 1) & (LANES - 1)) < n
    inv = jnp.exp(-cum)
    dend = jnp.exp(cend - cum)
    a_s[...] = _bf(-kkn * jnp.exp(cum - wlog))
    r_s[...] = _bf(r * jnp.exp(cum))
    b_s[...] = _bf(beta * inv)
    k_s[...] = _bf(k2 * inv)
    bh_s[...] = _bf(beta * dend)
    kh_s[...] = _bf(k2 * dend)
    vsw_s[...] = _bf(_swap_pair_halves(v, first_half))
    pe_s[...] = jnp.exp(cend)

    lane = lax.broadcasted_iota(jnp.int32, (c_len, LANES), 1)
    rowi = lax.broadcasted_iota(jnp.int32, (c_len, LANES), 0)
    m0f = lane < n
    m0b = jnp.where(m0f, 1.0, 0.0).astype(BF16)
    m1b = jnp.where(m0f, 0.0, 1.0).astype(BF16)
    sidx = lane & (n - 1)
    strict2 = sidx < rowi
    incl2 = sidx <= rowi
    r2 = lax.broadcasted_iota(jnp.int32, (LANES, LANES), 0)
    l2 = lax.broadcasted_iota(jnp.int32, (LANES, LANES), 1)
    bd = (r2 < n) == (l2 < n)
    eye2 = r2 == l2
    zer = jnp.zeros((c_len, LANES), BF16)
    n_mid_rounds = int(math.log2(c_len)) - 2

    n_chunks = tt // c_len
    hpp = LANES // n

    def group_body(gidx, carry):
        lanes = [pl.ds(pl.multiple_of((gidx * pp + pi) * LANES, LANES), LANES) for pi in range(pp)]
        slabs = [(pi, c) for pi in range(pp) for c in range(n_chunks)]
        chains = [(si, q) for si in range(len(slabs)) for q in range(hpp)]
        rows = [slice(c * c_len, (c + 1) * c_len) for c in range(n_chunks)]

        a_p = [a_s[rows[c], lanes[pi]] for pi, c in slabs]
        r_p = [r_s[rows[c], lanes[pi]] for pi, c in slabs]
        bh_p = [bh_s[rows[c], lanes[pi]] for pi, c in slabs]
        kh_p = [kh_s[rows[c], lanes[pi]] for pi, c in slabs]
        vsw_p = [vsw_s[rows[c], lanes[pi]] for pi, c in slabs]
        bk = [jnp.concatenate([b_s[rows[c], lanes[pi]], k_s[rows[c], lanes[pi]]], axis=0) for pi, c in slabs]
        msk = [(m0b, m1b) if q == 0 else (m1b, m0b) for _, q in chains]

        aq = [a_p[si] * msk[ci][0] for ci, (si, q) in enumerate(chains)]
        vq = [vsw_p[si] * msk[ci][1] for ci, (si, q) in enumerate(chains)]
        sc = [_dot_nt(jnp.concatenate([aq[ci], r_p[si] * msk[ci][0]], axis=0), bk[si])
              for ci, (si, q) in enumerate(chains)]
        top = [_bf(jnp.where(strict2, x[:c_len], 0.0)) for x in sc]
        bot = [_bf(jnp.where(incl2, x[c_len:], 0.0)) for x in sc]
        z = [aq[ci].astype(F32) + _dot(top[ci], jnp.concatenate([zer, vq[ci]], axis=0))
             for ci in range(len(chains))]
        t = [_dot(top[ci][:, :n], jnp.concatenate([_bf(z[ci]), top[ci]], axis=1)) for ci in range(len(chains))]
        z = [z[ci] + t[ci][:, :LANES] for ci in range(len(chains))]
        pb = [_bf(t[ci][:, LANES:]) for ci in range(len(chains))]
        for _ in range(n_mid_rounds):
            t = [_dot(pb[ci][:, :n], jnp.concatenate([_bf(z[ci]), pb[ci]], axis=1)) for ci in range(len(chains))]
            z = [z[ci] + t[ci][:, :LANES] for ci in range(len(chains))]
            pb = [_bf(t[ci][:, LANES:]) for ci in range(len(chains))]
        z = [z[ci] + _dot(pb[ci][:, :n], _bf(z[ci])) for ci in range(len(chains))]
        qv = [jnp.concatenate([_bf(z[ci]), vq[ci]], axis=0) for ci in range(len(chains))]
        ws = [_dot(bot[ci], qv[ci]) for ci in range(len(chains))]
        gq = [_dot_tn(qv[ci], jnp.concatenate([bh_p[si] * msk[ci][0], kh_p[si] * msk[ci][0]], axis=0))
              for ci, (si, q) in enumerate(chains)]

        rhat, m_bd, n_bd = [], [], []
        for si, (pi, c) in enumerate(slabs):
            w0, w1 = ws[si * hpp], ws[si * hpp + 1]
            mn = gq[si * hpp] + gq[si * hpp + 1]
            rhat.append(_bf(r_p[si].astype(F32) + jnp.where(m0f, w0, w1)))
            y0_s[rows[c], lanes[pi]] = jnp.where(m0f, w1, w0)
            pe_p = pe_s[c * c_len:c * c_len + 1, lanes[pi]]
            m_bd.append(_bf(jnp.where(bd, mn, 0.0) + jnp.where(eye2, pe_p, 0.0)))
            n_sw = jnp.where(bd, 0.0, mn)
            n_bd.append(jnp.concatenate([n_sw[n:], n_sw[:n]], axis=0))

        s_bd = [st_ref[gidx * pp + pi] for pi in range(pp)]
        for c in range(n_chunks):
            sb = [_bf(s_bd[pi]) for pi in range(pp)]
            for pi in range(pp):
                y_s[rows[c], lanes[pi]] = _dot_nt(rhat[pi * n_chunks + c], sb[pi])
            s_bd = [_dot(sb[pi], m_bd[pi * n_chunks + c]) + n_bd[pi * n_chunks + c] for pi in range(pp)]
        for pi in range(pp):
            st_ref[gidx * pp + pi] = s_bd[pi]
        return carry

    lax.fori_loop(0, pairs // pp, group_body, 0)

    y = y_s[...] + _swap_pair_halves(y0_s[...], first_half)
    mean = _segsum(y, ones_bd) * (1.0 / n)
    yc = y - mean
    var = _segsum(yc * yc, ones_bd) * (1.0 / n)
    yn = yc * lax.rsqrt(var + LN_X_EPS) * lw_ref[...] + lb_ref[...]
    o_ref[...] = ((yn + bonus) * g).astype(o_ref.dtype)


def _rwkv(z, mu_m, mu_l, w0, w2p, a0, a2p, g2p, k_k, k_a, r_k, lnx_w, lnx_b, *, width, s5w, name):
    b, t, zc = z.shape
    dlp, alp, glp = w2p.shape[0], a2p.shape[0], g2p.shape[0]
    lw = dlp + alp + glp
    tt = _pick(t, (128, 64))
    lora_blk = (3 * width + s5w) // lw
    assert lora_blk * lw == 3 * width + s5w
    pairs = width // LANES

    def row(x):
        return x.reshape(1, -1).astype(F32)

    vec = pl.BlockSpec((1, width), lambda bi, i: (0, 0))
    kern = functools.partial(_rwkv_kernel, tt=tt, width=width, dlp=dlp, alp=alp)
    return pl.pallas_call(
        kern,
        grid=(b, t // tt),
        in_specs=[
            pl.BlockSpec((None, tt, 3 * width), lambda bi, i: (bi, i, 0)),
            pl.BlockSpec((None, tt, lw), lambda bi, i: (bi, i, lora_blk)),
            pl.BlockSpec((1, 3 * width), lambda bi, i: (0, 0)),
            pl.BlockSpec((1, lw), lambda bi, i: (0, 0)),
            vec,
            pl.BlockSpec((dlp, width), lambda bi, i: (0, 0)),
            vec,
            pl.BlockSpec((alp, width), lambda bi, i: (0, 0)),
            pl.BlockSpec((glp, width), lambda bi, i: (0, 0)),
            vec, vec, vec, vec, vec,
        ],
        out_specs=pl.BlockSpec((None, tt, width), lambda bi, i: (bi, i, 0)),
        out_shape=jax.ShapeDtypeStruct((b, t, width), BF16),
        scratch_shapes=[
            pltpu.VMEM((pairs, LANES, LANES), F32),
            pltpu.VMEM((1, 3 * width), F32),
            pltpu.VMEM((1, lw), F32),
        ] + [pltpu.VMEM((tt, width), BF16)] * 7 + [pltpu.VMEM((tt, width), F32)] * 3,
        compiler_params=_cparams(("parallel", "arbitrary")),
        name=name,
    )(z, z, row(mu_m), row(mu_l), row(w0), w2p, row(a0), a2p, g2p,
      row(k_k), row(k_a), row(r_k), row(lnx_w), row(lnx_b))


def _gelu_tanh(x):
    return x * (0.5 * (1.0 + jnp.tanh(0.7978845608028654 * (x + 0.044715 * (x * x * x)))))


def _s5_kernel(u_ref, t_ref, w_ref, g_ref, m1_ref, m2_ref, o_ref, *, gb, nc, nlev):
    rows = u_ref.shape[1]
    cidx = lax.broadcasted_iota(jnp.int32, (rows, 1), 0) & (nc - 1)
    gs = range(gb)
    u = [u_ref[gi] for gi in gs]
    y_intra = [_dot(u[gi], t_ref[gi]) for gi in gs]
    x = [_dot(u[gi], w_ref[gi]) for gi in gs]
    half = x[0].shape[1] // 2
    for lev in range(nlev):
        sh = 1 << lev
        xs = [jnp.where(cidx >= sh, pltpu.roll(x[gi], sh, axis=0), 0.0) for gi in gs]
        xsw = [pltpu.roll(xs[gi], half, axis=1) for gi in gs]
        x = [x[gi] + m1_ref[gi, lev:lev + 1, :] * xs[gi] + m2_ref[gi, lev:lev + 1, :] * xsw[gi] for gi in gs]
    xp = [jnp.where(cidx >= 1, pltpu.roll(x[gi], 1, axis=0), 0.0) for gi in gs]
    for gi in gs:
        y = y_intra[gi] + _dot(_bf(xp[gi]), g_ref[gi])
        o_ref[gi] = _gelu_tanh(y).astype(o_ref.dtype)


def _s5_core(u_g, t_m, w_m, g_m, m1, m2, *, nc, name):
    groups, rows, lc = u_g.shape
    sp = w_m.shape[2]
    nlev = m1.shape[1]
    gb = _pick(groups, (8, 4, 2, 1))

    def spec(shape):
        return pl.BlockSpec((gb,) + shape, lambda i: (i, 0, 0))

    return pl.pallas_call(
        functools.partial(_s5_kernel, gb=gb, nc=nc, nlev=nlev),
        grid=(groups // gb,),
        in_specs=[spec((rows, lc)), spec((lc, lc)), spec((lc, sp)), spec((sp, lc)),
                  spec((nlev, sp)), spec((nlev, sp))],
        out_specs=spec((rows, lc)),
        out_shape=jax.ShapeDtypeStruct((groups, rows, lc), BF16),
        compiler_params=_cparams(("parallel",)),
        name=name,
    )(u_g, t_m, w_m, g_m, m1, m2)


def _s5_operators(lam_re, lam_im, log_step, b_re, b_im, c_re, c_im, d_skip, *, nc):
    groups, sp = lam_re.shape
    ch = b_re.shape[2]
    lc = S5_CHUNK
    lr = lam_re.astype(F32)
    li = lam_im.astype(F32)
    dt = jnp.exp(log_step.astype(F32))[:, None]

    def apow(nsteps):
        nsteps = jnp.asarray(nsteps, F32)[:, None, None]
        mag = jnp.exp(lr * dt * nsteps)
        ang = li * dt * nsteps
        return mag * jnp.cos(ang), mag * jnp.sin(ang)

    ar, ai = apow(jnp.arange(lc + 1))
    den = lr * lr + li * li
    q_re = ((ar[1] - 1.0) * lr + ai[1] * li) / den
    q_im = (ai[1] * lr - (ar[1] - 1.0) * li) / den
    bb_re = q_re[:, :, None] * b_re - q_im[:, :, None] * b_im
    bb_im = q_re[:, :, None] * b_im + q_im[:, :, None] * b_re
    ce_re = c_re[None] * ar[:, :, None, :] - c_im[None] * ai[:, :, None, :]
    ce_im = c_re[None] * ai[:, :, None, :] + c_im[None] * ar[:, :, None, :]
    hp = lax.Precision.HIGHEST
    taps = (jnp.einsum('tghp,gpk->tghk', ce_re[:lc], bb_re, precision=hp)
            - jnp.einsum('tghp,gpk->tghk', ce_im[:lc], bb_im, precision=hp))
    taps = jnp.concatenate([taps, jnp.zeros_like(taps[:1])], axis=0)
    s_i = jnp.arange(lc)[:, None]
    t_i = jnp.arange(lc)[None, :]
    lag = jnp.where(t_i >= s_i, t_i - s_i, lc)
    t5 = taps[lag]
    t_m = jnp.transpose(t5, (2, 0, 4, 1, 3))
    skip = (jnp.eye(lc, dtype=F32)[None, :, None, :, None]
            * jnp.eye(ch, dtype=F32)[None, None, :, None, :]
            * d_skip.reshape(groups, 1, 1, 1, ch).astype(F32))
    t_m = (t_m + skip).reshape(groups, lc * ch, lc * ch)
    arr = ar[lc - 1 - jnp.arange(lc)]
    air = ai[lc - 1 - jnp.arange(lc)]
    w_re = arr[:, :, :, None] * bb_re[None] - air[:, :, :, None] * bb_im[None]
    w_im = arr[:, :, :, None] * bb_im[None] + air[:, :, :, None] * bb_re[None]
    w_m = jnp.concatenate([w_re, w_im], axis=2)
    w_m = jnp.transpose(w_m, (1, 0, 3, 2)).reshape(groups, lc * ch, 2 * sp)
    g_re = jnp.transpose(ce_re[1:], (1, 3, 0, 2))
    g_im = jnp.transpose(ce_im[1:], (1, 3, 0, 2))
    g_m = jnp.concatenate([g_re, -g_im], axis=1).reshape(groups, 2 * sp, lc * ch)
    nlev = max(1, int(math.ceil(math.log2(nc))))
    mr, mi = apow(lc * (2.0 ** jnp.arange(nlev)))
    m1 = jnp.transpose(jnp.concatenate([mr, mr], axis=2), (1, 0, 2))
    m2 = jnp.transpose(jnp.concatenate([-mi, mi], axis=2), (1, 0, 2))
    return _bf(t_m), _bf(w_m), _bf(g_m), m1, m2


def _pad_rows(w, to):
    return jnp.pad(w, ((0, to - w.shape[0]), (0, 0)))


def _round_up(x, m):
    return (x + m - 1) // m * m


def kernel(x, p, g_mix_pre, w_in, mu, w0, w2, a0, a2, g2, k_k, k_a, r_k, lnx_w, lnx_b, lam_re, lam_im, log_step, b_re, b_im, c_re, c_im, d_skip, w_glu, b_glu, w_out, g_mix_post, g_ffn_pre, w_ff1, w_ff2, g_ffn_post, w_ple, g_ple_gate, w_ple_gate, g_ple_post):
    bsz, seq, d = x.shape
    width = w0.shape[-1]
    s5w = w_glu.shape[-1]
    dl, al, gl = w2.shape[1], a2.shape[1], g2.shape[1]
    dlp, alp, glp = _round_up(dl, LANES), _round_up(al, LANES), _round_up(gl, LANES)
    groups = s5w // S5_GROUP_CH
    nc = seq // S5_CHUNK
    m = bsz * seq
    o_wl = 3 * width
    o_al = o_wl + dl
    o_gl = o_al + al
    o_s5 = o_gl + gl

    def pad_cols(wm, to):
        return jnp.pad(wm, ((0, 0), (0, to - wm.shape[1])))

    def layer(carry, lp):
        h, hn = carry
        (p_i, g_next_i, wi, mu_i, w0_i, w2_i, a0_i, a2_i, g2_i, k_k_i, k_a_i, r_k_i, lnx_w_i, lnx_b_i,
         lam_re_i, lam_im_i, log_step_i, b_re_i, b_im_i, c_re_i, c_im_i, d_skip_i, w_glu_i, b_glu_i, w_out_i,
         g_mix_post_i, g_ffn_pre_i, w_ff1_i, w_ff2_i, g_ffn_post_i, w_ple_i, g_ple_gate_i, w_ple_gate_i,
         g_ple_post_i) = lp
        w_perm = _bf(jnp.concatenate([
            wi[:, :o_wl], wi[:, o_s5:],
            pad_cols(wi[:, o_wl:o_al], dlp), pad_cols(wi[:, o_al:o_gl], alp), pad_cols(wi[:, o_gl:o_s5], glp)],
            axis=1))
        mu_m = mu_i[:o_wl]
        mu_l = jnp.concatenate([jnp.pad(mu_i[o_wl:o_al], (0, dlp - dl)), jnp.pad(mu_i[o_al:o_gl], (0, alp - al)),
                                jnp.pad(mu_i[o_gl:o_s5], (0, glp - gl))])

        z = _mm_rows(hn, w_perm, epilogue="none", out_dtype=F32, name="in_proj")
        z3 = z.reshape(bsz, seq, z.shape[1])
        y_rwkv = _rwkv(z3, mu_m, mu_l, w0_i, _bf(_pad_rows(w2_i, dlp)), a0_i, _bf(_pad_rows(a2_i, alp)),
                       _bf(_pad_rows(g2_i, glp)), k_k_i, k_a_i, r_k_i, lnx_w_i, lnx_b_i,
                       width=width, s5w=s5w, name="rwkv")

        u = _bf(z3[:, :, o_wl:o_wl + s5w])
        u_g = jnp.transpose(u.reshape(bsz, nc, S5_CHUNK, groups, S5_GROUP_CH), (3, 0, 1, 2, 4))
        u_g = u_g.reshape(groups, bsz * nc, S5_CHUNK * S5_GROUP_CH)
        ops = _s5_operators(lam_re_i, lam_im_i, log_step_i, b_re_i, b_im_i, c_re_i, c_im_i, d_skip_i, nc=nc)
        y_g = _s5_core(u_g, *ops, nc=nc, name="s5")
        y_s = jnp.transpose(y_g.reshape(groups, bsz, nc, S5_CHUNK, S5_GROUP_CH), (1, 2, 3, 0, 4)).reshape(m, s5w)
        y_s5 = _mm_rows(y_s, _bf(w_glu_i), epilogue="glu", out_dtype=BF16, extra=(y_s, b_glu_i), name="glu")

        mix_in = jnp.concatenate([y_rwkv.reshape(m, width), y_s5], axis=1)
        mixed = _mm_kt(mix_in, _bf(w_out_i), name="out_proj")
        h, hn = _resid_norm(h, mixed, g_mix_post_i, g_ffn_pre_i, name="mix_post")

        hid = _mm_rows(hn, _bf(w_ff1_i), epilogue="relu2", out_dtype=BF16, name="ff1")
        f = _mm_kt(hid, _bf(w_ff2_i), name="ff2")
        h, hn = _resid_norm(h, f, g_ffn_post_i, g_ple_gate_i, name="ffn_post")

        ge = _mm_rows(hn, _bf(w_ple_gate_i), epilogue="ple", out_dtype=F32,
                      extra=(_bf(p_i.reshape(m, -1)), _bf(w_ple_i)), name="ple")
        h, hn = _resid_norm(h, ge, g_ple_post_i, g_next_i, name="ple_post")
        return (h, hn), None

    g_next = jnp.roll(g_mix_pre, -1, axis=0)
    stacked = (p, g_next, w_in, mu, w0, w2, a0, a2, g2, k_k, k_a, r_k, lnx_w, lnx_b, lam_re, lam_im, log_step,
               b_re, b_im, c_re, c_im, d_skip, w_glu, b_glu, w_out, g_mix_post, g_ffn_pre, w_ff1, w_ff2, g_ffn_post,
               w_ple, g_ple_gate, w_ple_gate, g_ple_post)
    h0 = x.reshape(m, d)
    hn0 = _norm_cast(h0, g_mix_pre[0], name="pre_norm")
    (h, _), _ = lax.scan(layer, (h0, hn0), stacked)
    return h.reshape(bsz, seq, d)
```

```python
import functools
import math

import jax
import jax.numpy as jnp
from jax import lax
from jax.experimental import pallas as pl
from jax.experimental.pallas import tpu as pltpu

F32 = jnp.float32
BF16 = jnp.bfloat16

V7X_VMEM_BYTES = 64 * 1024 * 1024
LANES = 128
MXU_DIM = 256

VMEM_LIMIT = V7X_VMEM_BYTES - 8 * 1024 * 1024

RMS_EPS = 1e-6
LN_X_EPS = 64e-5
HEAD_DIM = 64
S5_GROUP_CH = 16
S5_CHUNK = 16
RWKV_CHUNK = 64
RWKV_PAIRS_PER_STEP = 4


def _bf(x):
    return x.astype(BF16)


def _dot(a, b):
    return jnp.dot(a, b, preferred_element_type=F32)


def _dot_nt(a, b):
    return lax.dot_general(a, b, (((1,), (1,)), ((), ())), preferred_element_type=F32)


def _dot_tn(a, b):
    return lax.dot_general(a, b, (((0,), (0,)), ((), ())), preferred_element_type=F32)


def _sigmoid(x):
    return 1.0 / (1.0 + jnp.exp(-x))


def _cparams(sem):
    return pltpu.CompilerParams(dimension_semantics=sem, vmem_limit_bytes=VMEM_LIMIT)


def _pick(n, cands):
    for c in cands:
        if n % c == 0:
            return c
    return n


def _rms(x, g):
    ms = jnp.mean(x * x, axis=-1, keepdims=True)
    return (x * lax.rsqrt(ms + RMS_EPS)) * g


def _mm_rows_kernel(*refs, epilogue):
    if epilogue == "glu":
        a_ref, w_ref, y_ref, b_ref, o_ref = refs
    elif epilogue == "ple":
        a_ref, w_ref, p_ref, wp_ref, o_ref = refs
    else:
        a_ref, w_ref, o_ref = refs
    acc = _dot(a_ref[...], w_ref[...])
    if epilogue == "none":
        o_ref[...] = acc.astype(o_ref.dtype)
    elif epilogue == "relu2":
        r = jnp.maximum(acc, 0.0)
        o_ref[...] = (r * r).astype(o_ref.dtype)
    elif epilogue == "glu":
        o_ref[...] = (y_ref[...].astype(F32) * _sigmoid(acc + b_ref[...])).astype(o_ref.dtype)
    elif epilogue == "ple":
        e = _dot(p_ref[...], wp_ref[...])
        o_ref[...] = (_sigmoid(acc) * e).astype(o_ref.dtype)


def _mm_rows(a, w, *, epilogue, out_dtype, extra=(), name):
    m, k = a.shape
    n = w.shape[1]
    tm = _pick(m, (1024, 512, 256, 128))
    tn = _pick(n, (1024, 512, 256, 128))
    in_specs = [
        pl.BlockSpec((tm, k), lambda i, j: (i, 0)),
        pl.BlockSpec((k, tn), lambda i, j: (0, j)),
    ]
    args = [a, w]
    if epilogue == "glu":
        y, b = extra
        in_specs += [pl.BlockSpec((tm, tn), lambda i, j: (i, j)),
                     pl.BlockSpec((1, tn), lambda i, j: (0, j))]
        args += [y, b.reshape(1, n).astype(F32)]
    elif epilogue == "ple":
        p, wp = extra
        kp = p.shape[1]
        in_specs += [pl.BlockSpec((tm, kp), lambda i, j: (i, 0)),
                     pl.BlockSpec((kp, tn), lambda i, j: (0, j))]
        args += [p, wp]
    return pl.pallas_call(
        functools.partial(_mm_rows_kernel, epilogue=epilogue),
        grid=(m // tm, n // tn),
        in_specs=in_specs,
        out_specs=pl.BlockSpec((tm, tn), lambda i, j: (i, j)),
        out_shape=jax.ShapeDtypeStruct((m, n), out_dtype),
        compiler_params=_cparams(("parallel", "arbitrary")),
        name=name,
    )(*args)


def _mm_kt_kernel(*refs, slab, n_first):
    a_refs, (w_ref, o_ref) = refs[:-2], refs[-2:]
    n = o_ref.shape[1]
    kk = pl.program_id(1)

    def accumulate(a_ref, first):
        a = a_ref[...]
        for s in range(n // slab):
            sl = slice(s * slab, (s + 1) * slab)
            if first:
                o_ref[:, sl] = _dot(a, w_ref[:, sl])
            else:
                o_ref[:, sl] += _dot(a, w_ref[:, sl])

    pl.when(kk == 0)(lambda: accumulate(a_refs[0], True))
    pl.when((kk > 0) & (kk < n_first))(lambda: accumulate(a_refs[0], False))
    if len(a_refs) > 1:
        pl.when(kk >= n_first)(lambda: accumulate(a_refs[1], False))


def _mm_kt(a_list, w, *, name):
    m, k1 = a_list[0].shape
    assert all(a.shape == (m, k1) for a in a_list) and len(a_list) <= 2
    n = w.shape[1]
    tm = _pick(m, (1024, 512, 256, 128))
    tk = _pick(k1, (512, 256, 128))
    slab = _pick(n, (512, 256, 128))
    n_first = k1 // tk
    a_specs = [pl.BlockSpec((tm, tk), lambda i, kk: (i, jnp.minimum(kk, n_first - 1)))]
    if len(a_list) > 1:
        a_specs.append(pl.BlockSpec((tm, tk), lambda i, kk: (i, jnp.maximum(kk - n_first, 0))))
    return pl.pallas_call(
        functools.partial(_mm_kt_kernel, slab=slab, n_first=n_first),
        grid=(m // tm, len(a_list) * n_first),
        in_specs=a_specs + [pl.BlockSpec((tk, n), lambda i, kk: (kk, 0))],
        out_specs=pl.BlockSpec((tm, n), lambda i, kk: (i, 0)),
        out_shape=jax.ShapeDtypeStruct((m, n), F32),
        compiler_params=_cparams(("parallel", "arbitrary")),
        name=name,
    )(*a_list, w)


def _norm_cast_kernel(x_ref, g_ref, o_ref):
    o_ref[...] = _bf(_rms(x_ref[...], g_ref[...]))


def _norm_cast(x, g, *, name):
    m, d = x.shape
    tm = _pick(m, (256, 128))
    spec = pl.BlockSpec((tm, d), lambda i: (i, 0))
    return pl.pallas_call(
        _norm_cast_kernel,
        grid=(m // tm,),
        in_specs=[spec, pl.BlockSpec((1, d), lambda i: (0, 0))],
        out_specs=spec,
        out_shape=jax.ShapeDtypeStruct((m, d), BF16),
        compiler_params=_cparams(("parallel",)),
        name=name,
    )(x, g.reshape(1, d).astype(F32))


def _resid_norm_kernel(h_ref, f_ref, gp_ref, gn_ref, o_ref, on_ref):
    h = h_ref[...] + _rms(f_ref[...], gp_ref[...])
    o_ref[...] = h
    on_ref[...] = _bf(_rms(h, gn_ref[...]))


def _resid_norm(h, f, g_post, g_next, *, name):
    m, d = h.shape
    tm = _pick(m, (256, 128))
    spec = pl.BlockSpec((tm, d), lambda i: (i, 0))
    vec = pl.BlockSpec((1, d), lambda i: (0, 0))
    return pl.pallas_call(
        _resid_norm_kernel,
        grid=(m // tm,),
        in_specs=[spec, spec, vec, vec],
        out_specs=[spec, spec],
        out_shape=[jax.ShapeDtypeStruct((m, d), F32), jax.ShapeDtypeStruct((m, d), BF16)],
        compiler_params=_cparams(("parallel",)),
        name=name,
    )(h, f, g_post.reshape(1, d).astype(F32), g_next.reshape(1, d).astype(F32))


def _segsum(x, ones_bd):
    hi = _bf(x)
    lo = _bf(x - hi.astype(F32))
    outs = []
    for s in range(x.shape[1] // MXU_DIM):
        sl = slice(s * MXU_DIM, (s + 1) * MXU_DIM)
        outs.append(_dot(hi[:, sl], ones_bd) + _dot(lo[:, sl], ones_bd))
    return jnp.concatenate(outs, axis=1)


def _swap_pair_halves(x, first_half):
    w = x.shape[1]
    return jnp.where(first_half, pltpu.roll(x, w - HEAD_DIM, axis=1), pltpu.roll(x, HEAD_DIM, axis=1))


def _rwkv_kernel(zm_ref, zl_ref, mum_ref, mul_ref, w0_ref, w2_ref, a0_ref, a2_ref, g2_ref,
                 kk_ref, ka_ref, rk_ref, lw_ref, lb_ref,
                 o_ref,
                 st_ref, carm_ref, carl_ref,
                 a_s, r_s, b_s, k_s, bh_s, kh_s, vsw_s, pe_s, y_s, y0_s,
                 *, tt, width, dlp, alp):
    c_len = RWKV_CHUNK
    n = HEAD_DIM
    pairs = width // LANES
    pp = RWKV_PAIRS_PER_STEP if pairs % RWKV_PAIRS_PER_STEP == 0 else 1

    @pl.when(pl.program_id(1) == 0)
    def _():
        st_ref[...] = jnp.zeros_like(st_ref)
        carm_ref[...] = jnp.zeros_like(carm_ref)
        carl_ref[...] = jnp.zeros_like(carl_ref)

    row = lax.broadcasted_iota(jnp.int32, (tt, 1), 0)

    def shift_lerp(x, car_ref, mu):
        xs = pltpu.roll(x, 1, axis=0)
        xs = jnp.where(row == 0, car_ref[...], xs)
        car_ref[...] = x[tt - 1:tt, :]
        return x + (xs - x) * mu

    zm = shift_lerp(zm_ref[...], carm_ref, mum_ref[...])
    zl = shift_lerp(zl_ref[...], carl_ref, mul_ref[...])
    r = zm[:, :width]
    k = zm[:, width:2 * width]
    v = zm[:, 2 * width:]
    xw = zl[:, :dlp]
    xa = zl[:, dlp:dlp + alp]
    xg = zl[:, dlp + alp:]

    wr = w0_ref[...] + _dot(_bf(jnp.tanh(xw)), w2_ref[...])
    softplus_neg = jnp.maximum(-wr, 0.0) + jnp.log(1.0 + jnp.exp(-jnp.abs(wr)))
    wlog = -jnp.exp(-softplus_neg - 0.5)
    a = _sigmoid(a0_ref[...] + _dot(_bf(xa), a2_ref[...]))
    g = _dot(_bf(_sigmoid(xg)), g2_ref[...])

    ri = lax.broadcasted_iota(jnp.int32, (MXU_DIM, MXU_DIM), 0)
    ci = lax.broadcasted_iota(jnp.int32, (MXU_DIM, MXU_DIM), 1)
    shift = int(math.log2(n))
    ones_bd = jnp.where((ri >> shift) == (ci >> shift), 1.0, 0.0).astype(BF16)

    kkv = k * kk_ref[...]
    ssq = _segsum(kkv * kkv, ones_bd)
    kkn = kkv / jnp.maximum(jnp.sqrt(ssq), 1e-12)
    k2 = k * (1.0 + (a - 1.0) * ka_ref[...])
    beta = kkn * a
    bonus = _segsum(r * k2 * rk_ref[...], ones_bd) * v

    rt_i = lax.broadcasted_iota(jnp.int32, (tt, tt), 0)
    ct_i = lax.broadcasted_iota(jnp.int32, (tt, tt), 1)
    cshift = int(math.log2(c_len))
    same = (rt_i >> cshift) == (ct_i >> cshift)
    lt_bd = jnp.where(same & (ct_i <= rt_i), 1.0, 0.0).astype(BF16)
    le_bd = jnp.where(same, 1.0, 0.0).astype(BF16)
    w_hi = _bf(wlog)
    w_lo = _bf(wlog - w_hi.astype(F32))
    cum = _dot(lt_bd, w_hi) + _dot(lt_bd, w_lo)
    cend = _dot(le_bd, w_hi) + _dot(le_bd, w_lo)

    first_half = (lax.broadcasted_iota(jnp.int32, (1, width), 1) & (LANES - 1)) < n
    inv = jnp.exp(-cum)
    dend = jnp.exp(cend - cum)
    a_s[...] = _bf(-kkn * jnp.exp(cum - wlog))
    r_s[...] = _bf(r * jnp.exp(cum))
    b_s[...] = _bf(beta * inv)
    k_s[...] = _bf(k2 * inv)
    bh_s[...] = _bf(beta * dend)
    kh_s[...] = _bf(k2 * dend)
    vsw_s[...] = _bf(_swap_pair_halves(v, first_half))
    pe_s[...] = jnp.exp(cend)

    lane = lax.broadcasted_iota(jnp.int32, (c_len, LANES), 1)
    rowi = lax.broadcasted_iota(jnp.int32, (c_len, LANES), 0)
    m0f = lane < n
    m0b = jnp.where(m0f, 1.0, 0.0).astype(BF16)
    m1b = jnp.where(m0f, 0.0, 1.0).astype(BF16)
    sidx = lane & (n - 1)
    strict2 = sidx < rowi
    incl2 = sidx <= rowi
    r2 = lax.broadcasted_iota(jnp.int32, (LANES, LANES), 0)
    l2 = lax.broadcasted_iota(jnp.int32, (LANES, LANES), 1)
    bd = (r2 < n) == (l2 < n)
    eye2 = r2 == l2
    zer = jnp.zeros((c_len, LANES), BF16)
    n_mid_rounds = int(math.log2(c_len)) - 2

    n_chunks = tt // c_len
    hpp = LANES // n

    def group_body(gidx, carry):
        lanes = [pl.ds(pl.multiple_of((gidx * pp + pi) * LANES, LANES), LANES) for pi in range(pp)]
        slabs = [(pi, c) for pi in range(pp) for c in range(n_chunks)]
        chains = [(si, q) for si in range(len(slabs)) for q in range(hpp)]
        rows = [slice(c * c_len, (c + 1) * c_len) for c in range(n_chunks)]

        a_p = [a_s[rows[c], lanes[pi]] for pi, c in slabs]
        r_p = [r_s[rows[c], lanes[pi]] for pi, c in slabs]
        bh_p = [bh_s[rows[c], lanes[pi]] for pi, c in slabs]
        kh_p = [kh_s[rows[c], lanes[pi]] for pi, c in slabs]
        vsw_p = [vsw_s[rows[c], lanes[pi]] for pi, c in slabs]
        bk = [jnp.concatenate([b_s[rows[c], lanes[pi]], k_s[rows[c], lanes[pi]]], axis=0) for pi, c in slabs]
        msk = [(m0b, m1b) if q == 0 else (m1b, m0b) for _, q in chains]

        aq = [a_p[si] * msk[ci][0] for ci, (si, q) in enumerate(chains)]
        vq = [vsw_p[si] * msk[ci][1] for ci, (si, q) in enumerate(chains)]
        sc = [_dot_nt(jnp.concatenate([aq[ci], r_p[si] * msk[ci][0]], axis=0), bk[si])
              for ci, (si, q) in enumerate(chains)]
        top = [_bf(jnp.where(strict2, x[:c_len], 0.0)) for x in sc]
        bot = [_bf(jnp.where(incl2, x[c_len:], 0.0)) for x in sc]
        z = [aq[ci].astype(F32) + _dot(top[ci], jnp.concatenate([zer, vq[ci]], axis=0))
             for ci in range(len(chains))]
        t = [_dot(top[ci][:, :n], jnp.concatenate([_bf(z[ci]), top[ci]], axis=1)) for ci in range(len(chains))]
        z = [z[ci] + t[ci][:, :LANES] for ci in range(len(chains))]
        pb = [_bf(t[ci][:, LANES:]) for ci in range(len(chains))]
        for _ in range(n_mid_rounds):
            t = [_dot(pb[ci][:, :n], jnp.concatenate([_bf(z[ci]), pb[ci]], axis=1)) for ci in range(len(chains))]
            z = [z[ci] + t[ci][:, :LANES] for ci in range(len(chains))]
            pb = [_bf(t[ci][:, LANES:]) for ci in range(len(chains))]
        z = [z[ci] + _dot(pb[ci][:, :n], _bf(z[ci])) for ci in range(len(chains))]
        qv = [jnp.concatenate([_bf(z[ci]), vq[ci]], axis=0) for ci in range(len(chains))]
        ws = [_dot(bot[ci], qv[ci]) for ci in range(len(chains))]
        gq = [_dot_tn(qv[ci], jnp.concatenate([bh_p[si] * msk[ci][0], kh_p[si] * msk[ci][0]], axis=0))
              for ci, (si, q) in enumerate(chains)]

        rhat, m_bd, n_bd = [], [], []
        for si, (pi, c) in enumerate(slabs):
            w0, w1 = ws[si * hpp], ws[si * hpp + 1]
            mn = gq[si * hpp] + gq[si * hpp + 1]
            rhat.append(_bf(r_p[si].astype(F32) + jnp.where(m0f, w0, w1)))
            y0_s[rows[c], lanes[pi]] = jnp.where(m0f, w1, w0)
            pe_p = pe_s[c * c_len:c * c_len + 1, lanes[pi]]
            m_bd.append(_bf(jnp.where(bd, mn, 0.0) + jnp.where(eye2, pe_p, 0.0)))
            n_sw = jnp.where(bd, 0.0, mn)
            n_bd.append(jnp.concatenate([n_sw[n:], n_sw[:n]], axis=0))

        s_bd = [st_ref[gidx * pp + pi] for pi in range(pp)]
        for c in range(n_chunks):
            sb = [_bf(s_bd[pi]) for pi in range(pp)]
            for pi in range(pp):
                y_s[rows[c], lanes[pi]] = _dot_nt(rhat[pi * n_chunks + c], sb[pi])
            s_bd = [_dot(sb[pi], m_bd[pi * n_chunks + c]) + n_bd[pi * n_chunks + c] for pi in range(pp)]
        for pi in range(pp):
            st_ref[gidx * pp + pi] = s_bd[pi]
        return carry

    lax.fori_loop(0, pairs // pp, group_body, 0)

    y = y_s[...] + _swap_pair_halves(y0_s[...], first_half)
    mean = _segsum(y, ones_bd) * (1.0 / n)
    yc = y - mean
    var = _segsum(yc * yc, ones_bd) * (1.0 / n)
    yn = yc * lax.rsqrt(var + LN_X_EPS) * lw_ref[...] + lb_ref[...]
    o_ref[...] = ((yn + bonus) * g).astype(o_ref.dtype)


def _rwkv(z, mu_m, mu_l, w0, w2p, a0, a2p, g2p, k_k, k_a, r_k, lnx_w, lnx_b, *, width, s5w, name):
    b, t, zc = z.shape
    dlp, alp, glp = w2p.shape[0], a2p.shape[0], g2p.shape[0]
    lw = dlp + alp + glp
    tt = _pick(t, (128, 64))
    lora_blk = (3 * width + s5w) // lw
    assert lora_blk * lw == 3 * width + s5w
    pairs = width // LANES

    def row(x):
        return x.reshape(1, -1).astype(F32)

    vec = pl.BlockSpec((1, width), lambda bi, i: (0, 0))
    kern = functools.partial(_rwkv_kernel, tt=tt, width=width, dlp=dlp, alp=alp)
    return pl.pallas_call(
        kern,
        grid=(b, t // tt),
        in_specs=[
            pl.BlockSpec((None, tt, 3 * width), lambda bi, i: (bi, i, 0)),
            pl.BlockSpec((None, tt, lw), lambda bi, i: (bi, i, lora_blk)),
            pl.BlockSpec((1, 3 * width), lambda bi, i: (0, 0)),
            pl.BlockSpec((1, lw), lambda bi, i: (0, 0)),
            vec,
            pl.BlockSpec((dlp, width), lambda bi, i: (0, 0)),
            vec,
            pl.BlockSpec((alp, width), lambda bi, i: (0, 0)),
            pl.BlockSpec((glp, width), lambda bi, i: (0, 0)),
            vec, vec, vec, vec, vec,
        ],
        out_specs=pl.BlockSpec((None, tt, width), lambda bi, i: (bi, i, 0)),
        out_shape=jax.ShapeDtypeStruct((b, t, width), BF16),
        scratch_shapes=[
            pltpu.VMEM((pairs, LANES, LANES), F32),
            pltpu.VMEM((1, 3 * width), F32),
            pltpu.VMEM((1, lw), F32),
        ] + [pltpu.VMEM((tt, width), BF16)] * 7 + [pltpu.VMEM((tt, width), F32)] * 3,
        compiler_params=_cparams(("parallel", "arbitrary")),
        name=name,
    )(z, z, row(mu_m), row(mu_l), row(w0), w2p, row(a0), a2p, g2p,
      row(k_k), row(k_a), row(r_k), row(lnx_w), row(lnx_b))


def _gelu_tanh(x):
    return x * (0.5 * (1.0 + jnp.tanh(0.7978845608028654 * (x + 0.044715 * (x * x * x)))))


def _s5_kernel(u_ref, bd_ref, wz_ref, gz_ref, m1_ref, m2_ref, o_ref, bdw_s, xp_s, *, nlev):
    lc, nc, _ = u_ref.shape
    gpb, ch, sp2 = wz_ref.shape[1:]
    half = sp2 // 2

    y_intra = []
    for t in range(lc):
        acc = _dot(u_ref[0], bd_ref[t])
        for s in range(1, t + 1):
            acc = acc + _dot(u_ref[s], bd_ref[t - s])
        y_intra.append(acc)

    bdw_s[...] = jnp.zeros_like(bdw_s)
    z = None
    for s in range(lc):
        buf = bdw_s.at[s % 2]
        for g in range(gpb):
            buf[g * ch:(g + 1) * ch, g * sp2:(g + 1) * sp2] = wz_ref[s, g]
        d = _dot(u_ref[s], buf[...])
        z = d if z is None else z + d

    cidx = lax.broadcasted_iota(jnp.int32, (nc, 1), 0)
    gs = range(gpb)
    x = [z[:, g * sp2:(g + 1) * sp2] for g in gs]
    for lev in range(nlev):
        sh = 1 << lev
        xs = [jnp.where(cidx >= sh, pltpu.roll(x[g], sh, axis=0), 0.0) for g in gs]
        xsw = [pltpu.roll(xs[g], half, axis=1) for g in gs]
        x = [x[g] + m1_ref[lev:lev + 1, g * sp2:(g + 1) * sp2] * xs[g]
             + m2_ref[lev:lev + 1, g * sp2:(g + 1) * sp2] * xsw[g] for g in gs]
    for g in gs:
        xp_s[:, g * sp2:(g + 1) * sp2] = _bf(jnp.where(cidx >= 1, pltpu.roll(x[g], 1, axis=0), 0.0))

    xp = xp_s[...]
    for t in range(lc):
        buf = bdw_s.at[t % 2]
        for g in range(gpb):
            buf[g * ch:(g + 1) * ch, g * sp2:(g + 1) * sp2] = gz_ref[t, g]
        y = y_intra[t] + _dot_nt(xp, buf[...])
        o_ref[t] = _gelu_tanh(y).astype(o_ref.dtype)


def _s5_core(u_p, bd, wz, gz, m1, m2, *, name):
    bsz, lc, nc, s5w = u_p.shape
    nblk, _, gpb, ch, sp2 = wz.shape
    nlev = m1.shape[1]
    return pl.pallas_call(
        functools.partial(_s5_kernel, nlev=nlev),
        grid=(bsz, nblk),
        in_specs=[
            pl.BlockSpec((None, lc, nc, MXU_DIM), lambda b, j: (b, 0, 0, j)),
            pl.BlockSpec((None, lc, MXU_DIM, MXU_DIM), lambda b, j: (j, 0, 0, 0)),
            pl.BlockSpec((None, lc, gpb, ch, sp2), lambda b, j: (j, 0, 0, 0, 0)),
            pl.BlockSpec((None, lc, gpb, ch, sp2), lambda b, j: (j, 0, 0, 0, 0)),
            pl.BlockSpec((None, nlev, gpb * sp2), lambda b, j: (j, 0, 0)),
            pl.BlockSpec((None, nlev, gpb * sp2), lambda b, j: (j, 0, 0)),
        ],
        out_specs=pl.BlockSpec((None, lc, nc, MXU_DIM), lambda b, j: (b, 0, 0, j)),
        out_shape=jax.ShapeDtypeStruct((bsz, lc, nc, s5w), BF16),
        scratch_shapes=[pltpu.VMEM((2, MXU_DIM, gpb * sp2), BF16), pltpu.VMEM((nc, gpb * sp2), BF16)],
        compiler_params=_cparams(("parallel", "arbitrary")),
        name=name,
    )(u_p, bd, wz, gz, m1, m2)


def _s5_operators(lam_re, lam_im, log_step, b_re, b_im, c_re, c_im, d_skip, *, nc):
    groups, sp = lam_re.shape
    ch = b_re.shape[2]
    lc = S5_CHUNK
    lr = lam_re.astype(F32)
    li = lam_im.astype(F32)
    dt = jnp.exp(log_step.astype(F32))[:, None]

    def apow(nsteps):
        nsteps = jnp.asarray(nsteps, F32)[:, None, None]
        mag = jnp.exp(lr * dt * nsteps)
        ang = li * dt * nsteps
        return mag * jnp.cos(ang), mag * jnp.sin(ang)

    ar, ai = apow(jnp.arange(lc + 1))
    den = lr * lr + li * li
    q_re = ((ar[1] - 1.0) * lr + ai[1] * li) / den
    q_im = (ai[1] * lr - (ar[1] - 1.0) * li) / den
    bb_re = q_re[:, :, None] * b_re - q_im[:, :, None] * b_im
    bb_im = q_re[:, :, None] * b_im + q_im[:, :, None] * b_re
    ce_re = c_re[None] * ar[:, :, None, :] - c_im[None] * ai[:, :, None, :]
    ce_im = c_re[None] * ai[:, :, None, :] + c_im[None] * ar[:, :, None, :]
    gpb = MXU_DIM // ch
    nblk = groups // gpb
    taps = jnp.sum(ce_re[:lc, :, None, :, :] * jnp.transpose(bb_re, (0, 2, 1))[None, :, :, None, :]
                   - ce_im[:lc, :, None, :, :] * jnp.transpose(bb_im, (0, 2, 1))[None, :, :, None, :], axis=-1)
    skip = jnp.eye(ch, dtype=F32)[None] * d_skip.reshape(groups, 1, ch).astype(F32)
    taps = taps.at[0].add(skip)
    taps = jnp.transpose(taps.reshape(lc, nblk, gpb, ch, ch), (1, 0, 2, 3, 4))
    bd = (taps[:, :, :, :, None, :] * jnp.eye(gpb, dtype=F32)[None, None, :, None, :, None])
    bd = bd.reshape(nblk, lc, MXU_DIM, MXU_DIM)
    arr = ar[lc - 1 - jnp.arange(lc)]
    air = ai[lc - 1 - jnp.arange(lc)]
    w_re = arr[:, :, :, None] * bb_re[None] - air[:, :, :, None] * bb_im[None]
    w_im = arr[:, :, :, None] * bb_im[None] + air[:, :, :, None] * bb_re[None]
    wz = jnp.transpose(jnp.concatenate([w_re, w_im], axis=2), (1, 0, 3, 2))
    wz = jnp.transpose(wz.reshape(nblk, gpb, lc, ch, 2 * sp), (0, 2, 1, 3, 4))
    gz = jnp.transpose(jnp.concatenate([ce_re[1:], -ce_im[1:]], axis=3), (1, 0, 2, 3))
    gz = jnp.transpose(gz.reshape(nblk, gpb, lc, ch, 2 * sp), (0, 2, 1, 3, 4))
    nlev = max(1, int(math.ceil(math.log2(nc))))
    mr, mi = apow(lc * (2.0 ** jnp.arange(nlev)))

    def per_block(a):
        return jnp.transpose(a.reshape(nlev, nblk, gpb * 2 * sp), (1, 0, 2))

    m1 = per_block(jnp.concatenate([mr, mr], axis=2))
    m2 = per_block(jnp.concatenate([-mi, mi], axis=2))
    return _bf(bd), _bf(wz), _bf(gz), m1, m2


def _pad_rows(w, to):
    return jnp.pad(w, ((0, to - w.shape[0]), (0, 0)))


def _round_up(x, m):
    return (x + m - 1) // m * m


def kernel(x, p, g_mix_pre, w_in, mu, w0, w2, a0, a2, g2, k_k, k_a, r_k, lnx_w, lnx_b, lam_re, lam_im, log_step, b_re, b_im, c_re, c_im, d_skip, w_glu, b_glu, w_out, g_mix_post, g_ffn_pre, w_ff1, w_ff2, g_ffn_post, w_ple, g_ple_gate, w_ple_gate, g_ple_post):
    bsz, seq, d = x.shape
    width = w0.shape[-1]
    s5w = w_glu.shape[-1]
    dl, al, gl = w2.shape[1], a2.shape[1], g2.shape[1]
    dlp, alp, glp = _round_up(dl, LANES), _round_up(al, LANES), _round_up(gl, LANES)
    nc = seq // S5_CHUNK
    m = bsz * seq
    o_wl = 3 * width
    o_al = o_wl + dl
    o_gl = o_al + al
    o_s5 = o_gl + gl

    def pad_cols(wm, to):
        return jnp.pad(wm, ((0, 0), (0, to - wm.shape[1])))

    def layer(carry, lp):
        h, hn = carry
        (p_i, g_next_i, wi, mu_i, w0_i, w2_i, a0_i, a2_i, g2_i, k_k_i, k_a_i, r_k_i, lnx_w_i, lnx_b_i,
         lam_re_i, lam_im_i, log_step_i, b_re_i, b_im_i, c_re_i, c_im_i, d_skip_i, w_glu_i, b_glu_i, w_out_i,
         g_mix_post_i, g_ffn_pre_i, w_ff1_i, w_ff2_i, g_ffn_post_i, w_ple_i, g_ple_gate_i, w_ple_gate_i,
         g_ple_post_i) = lp
        w_perm = _bf(jnp.concatenate([
            wi[:, :o_wl], wi[:, o_s5:],
            pad_cols(wi[:, o_wl:o_al], dlp), pad_cols(wi[:, o_al:o_gl], alp), pad_cols(wi[:, o_gl:o_s5], glp)],
            axis=1))
        mu_m = mu_i[:o_wl]
        mu_l = jnp.concatenate([jnp.pad(mu_i[o_wl:o_al], (0, dlp - dl)), jnp.pad(mu_i[o_al:o_gl], (0, alp - al)),
                                jnp.pad(mu_i[o_gl:o_s5], (0, glp - gl))])

        z = _mm_rows(hn, w_perm, epilogue="none", out_dtype=F32, name="in_proj")
        z3 = z.reshape(bsz, seq, z.shape[1])
        y_rwkv = _rwkv(z3, mu_m, mu_l, w0_i, _bf(_pad_rows(w2_i, dlp)), a0_i, _bf(_pad_rows(a2_i, alp)),
                       _bf(_pad_rows(g2_i, glp)), k_k_i, k_a_i, r_k_i, lnx_w_i, lnx_b_i,
                       width=width, s5w=s5w, name="rwkv")

        u = _bf(z3[:, :, o_wl:o_wl + s5w])
        u_p = jnp.transpose(u.reshape(bsz, nc, S5_CHUNK, s5w), (0, 2, 1, 3))
        ops = _s5_operators(lam_re_i, lam_im_i, log_step_i, b_re_i, b_im_i, c_re_i, c_im_i, d_skip_i, nc=nc)
        y_p = _s5_core(u_p, *ops, name="s5")
        y_s = jnp.transpose(y_p, (0, 2, 1, 3)).reshape(m, s5w)
        y_s5 = _mm_rows(y_s, _bf(w_glu_i), epilogue="glu", out_dtype=BF16, extra=(y_s, b_glu_i), name="glu")

        if width == s5w:
            mix_in = [y_rwkv.reshape(m, width), y_s5]
        else:
            mix_in = [jnp.concatenate([y_rwkv.reshape(m, width), y_s5], axis=1)]
        mixed = _mm_kt(mix_in, _bf(w_out_i), name="out_proj")
        h, hn = _resid_norm(h, mixed, g_mix_post_i, g_ffn_pre_i, name="mix_post")

        hid = _mm_rows(hn, _bf(w_ff1_i), epilogue="relu2", out_dtype=BF16, name="ff1")
        f = _mm_kt([hid], _bf(w_ff2_i), name="ff2")
        h, hn = _resid_norm(h, f, g_ffn_post_i, g_ple_gate_i, name="ffn_post")

        ge = _mm_rows(hn, _bf(w_ple_gate_i), epilogue="ple", out_dtype=F32,
                      extra=(_bf(p_i.reshape(m, -1)), _bf(w_ple_i)), name="ple")
        h, hn = _resid_norm(h, ge, g_ple_post_i, g_next_i, name="ple_post")
        return (h, hn), None

    g_next = jnp.roll(g_mix_pre, -1, axis=0)
    stacked = (p, g_next, w_in, mu, w0, w2, a0, a2, g2, k_k, k_a, r_k, lnx_w, lnx_b, lam_re, lam_im, log_step,
               b_re, b_im, c_re, c_im, d_skip, w_glu, b_glu, w_out, g_mix_post, g_ffn_pre, w_ff1, w_ff2, g_ffn_post,
               w_ple, g_ple_gate, w_ple_gate, g_ple_post)
    h0 = x.reshape(m, d)
    hn0 = _norm_cast(h0, g_mix_pre[0], name="pre_norm")
    carry = (h0, hn0)
    for i in range(w_in.shape[0]):
        carry, _ = layer(carry, tuple(a[i] for a in stacked))
    return carry[0].reshape(bsz, seq, d)
```

```python
import functools
import math

import jax
import jax.numpy as jnp
from jax import lax
from jax.experimental import pallas as pl
from jax.experimental.pallas import tpu as pltpu

F32 = jnp.float32
BF16 = jnp.bfloat16

V7X_VMEM_BYTES = 64 * 1024 * 1024
LANES = 128
MXU_DIM = 256

VMEM_LIMIT = V7X_VMEM_BYTES - 8 * 1024 * 1024

RMS_EPS = 1e-6
LN_X_EPS = 64e-5
HEAD_DIM = 64
S5_GROUP_CH = 16
S5_CHUNK = 16
RWKV_CHUNK = 64
RWKV_PAIRS_PER_STEP = 4


def _bf(x):
    return x.astype(BF16)


def _dot(a, b):
    return jnp.dot(a, b, preferred_element_type=F32)


def _dot_nt(a, b):
    return lax.dot_general(a, b, (((1,), (1,)), ((), ())), preferred_element_type=F32)


def _dot_tn(a, b):
    return lax.dot_general(a, b, (((0,), (0,)), ((), ())), preferred_element_type=F32)


def _sigmoid(x):
    return 1.0 / (1.0 + jnp.exp(-x))


def _cparams(sem):
    return pltpu.CompilerParams(dimension_semantics=sem, vmem_limit_bytes=VMEM_LIMIT)


def _pick(n, cands):
    for c in cands:
        if n % c == 0:
            return c
    return n


def _rms(x, g):
    ms = jnp.mean(x * x, axis=-1, keepdims=True)
    return (x * lax.rsqrt(ms + RMS_EPS)) * g


def _mm_rows_kernel(*refs, epilogue):
    if epilogue == "glu":
        a_ref, w_ref, y_ref, b_ref, o_ref = refs
    elif epilogue == "ple":
        a_ref, w_ref, p_ref, wp_ref, o_ref = refs
    else:
        a_ref, w_ref, o_ref = refs
    acc = _dot(a_ref[...], w_ref[...])
    if epilogue == "none":
        o_ref[...] = acc.astype(o_ref.dtype)
    elif epilogue == "relu2":
        r = jnp.maximum(acc, 0.0)
        o_ref[...] = (r * r).astype(o_ref.dtype)
    elif epilogue == "glu":
        o_ref[...] = (y_ref[...].astype(F32) * _sigmoid(acc + b_ref[...])).astype(o_ref.dtype)
    elif epilogue == "ple":
        e = _dot(p_ref[...], wp_ref[...])
        o_ref[...] = (_sigmoid(acc) * e).astype(o_ref.dtype)


def _mm_rows(a, w, *, epilogue, out_dtype, extra=(), name):
    m, k = a.shape
    n = w.shape[1]
    tm = _pick(m, (1024, 512, 256, 128))
    tn = _pick(n, (1024, 512, 256, 128))
    in_specs = [
        pl.BlockSpec((tm, k), lambda i, j: (i, 0)),
        pl.BlockSpec((k, tn), lambda i, j: (0, j)),
    ]
    args = [a, w]
    if epilogue == "glu":
        y, b = extra
        in_specs += [pl.BlockSpec((tm, tn), lambda i, j: (i, j)),
                     pl.BlockSpec((1, tn), lambda i, j: (0, j))]
        args += [y, b.reshape(1, n).astype(F32)]
    elif epilogue == "ple":
        p, wp = extra
        kp = p.shape[1]
        in_specs += [pl.BlockSpec((tm, kp), lambda i, j: (i, 0)),
                     pl.BlockSpec((kp, tn), lambda i, j: (0, j))]
        args += [p, wp]
    return pl.pallas_call(
        functools.partial(_mm_rows_kernel, epilogue=epilogue),
        grid=(m // tm, n // tn),
        in_specs=in_specs,
        out_specs=pl.BlockSpec((tm, tn), lambda i, j: (i, j)),
        out_shape=jax.ShapeDtypeStruct((m, n), out_dtype),
        compiler_params=_cparams(("parallel", "arbitrary")),
        name=name,
    )(*args)


def _mm_kt_kernel(*refs, slab, n_first):
    a_refs, (w_ref, o_ref, acc_ref) = refs[:-3], refs[-3:]
    n = o_ref.shape[1]
    kk = pl.program_id(1)

    def accumulate(a_ref, first):
        a = a_ref[...]
        for s in range(n // slab):
            sl = slice(s * slab, (s + 1) * slab)
            prod = _dot(a, w_ref[:, sl])
            if first:
                acc_ref[:, sl] = prod
            else:
                acc_ref[:, sl] += prod

    pl.when(kk == 0)(lambda: accumulate(a_refs[0], True))
    pl.when((kk > 0) & (kk < n_first))(lambda: accumulate(a_refs[0], False))
    if len(a_refs) > 1:
        pl.when(kk >= n_first)(lambda: accumulate(a_refs[1], False))

    @pl.when(kk == pl.num_programs(1) - 1)
    def _():
        for s in range(n // slab):
            sl = slice(s * slab, (s + 1) * slab)
            o_ref[:, sl] = acc_ref[:, sl].astype(o_ref.dtype)


def _mm_kt(a_list, w, *, name):
    m, k1 = a_list[0].shape
    assert all(a.shape == (m, k1) for a in a_list) and len(a_list) <= 2
    n = w.shape[1]
    tm = _pick(m, (1024, 512, 256, 128))
    tk = _pick(k1, (512, 256, 128))
    slab = _pick(n, (512, 256, 128))
    n_first = k1 // tk
    a_specs = [pl.BlockSpec((tm, tk), lambda i, kk: (i, jnp.minimum(kk, n_first - 1)))]
    if len(a_list) > 1:
        a_specs.append(pl.BlockSpec((tm, tk), lambda i, kk: (i, jnp.maximum(kk - n_first, 0))))
    return pl.pallas_call(
        functools.partial(_mm_kt_kernel, slab=slab, n_first=n_first),
        grid=(m // tm, len(a_list) * n_first),
        in_specs=a_specs + [pl.BlockSpec((tk, n), lambda i, kk: (kk, 0))],
        out_specs=pl.BlockSpec((tm, n), lambda i, kk: (i, 0)),
        out_shape=jax.ShapeDtypeStruct((m, n), BF16),
        scratch_shapes=[pltpu.VMEM((tm, n), F32)],
        compiler_params=_cparams(("parallel", "arbitrary")),
        name=name,
    )(*a_list, w)


def _norm_cast_kernel(x_ref, g_ref, o_ref):
    o_ref[...] = _bf(_rms(x_ref[...], g_ref[...]))


def _norm_cast(x, g, *, name):
    m, d = x.shape
    tm = _pick(m, (256, 128))
    spec = pl.BlockSpec((tm, d), lambda i: (i, 0))
    return pl.pallas_call(
        _norm_cast_kernel,
        grid=(m // tm,),
        in_specs=[spec, pl.BlockSpec((1, d), lambda i: (0, 0))],
        out_specs=spec,
        out_shape=jax.ShapeDtypeStruct((m, d), BF16),
        compiler_params=_cparams(("parallel",)),
        name=name,
    )(x, g.reshape(1, d).astype(F32))


def _resid_norm_kernel(h_ref, f_ref, gp_ref, gn_ref, o_ref, on_ref):
    h = h_ref[...] + _rms(f_ref[...].astype(F32), gp_ref[...])
    o_ref[...] = h
    on_ref[...] = _bf(_rms(h, gn_ref[...]))


def _resid_norm(h, f, g_post, g_next, *, name):
    m, d = h.shape
    tm = _pick(m, (256, 128))
    spec = pl.BlockSpec((tm, d), lambda i: (i, 0))
    vec = pl.BlockSpec((1, d), lambda i: (0, 0))
    return pl.pallas_call(
        _resid_norm_kernel,
        grid=(m // tm,),
        in_specs=[spec, spec, vec, vec],
        out_specs=[spec, spec],
        out_shape=[jax.ShapeDtypeStruct((m, d), F32), jax.ShapeDtypeStruct((m, d), BF16)],
        compiler_params=_cparams(("parallel",)),
        name=name,
    )(h, f, g_post.reshape(1, d).astype(F32), g_next.reshape(1, d).astype(F32))


def _segsum(x, ones_bd, split=False):
    hi = _bf(x)
    lo = _bf(x - hi.astype(F32)) if split else None
    outs = []
    for s in range(x.shape[1] // MXU_DIM):
        sl = slice(s * MXU_DIM, (s + 1) * MXU_DIM)
        acc = _dot(hi[:, sl], ones_bd)
        if split:
            acc = acc + _dot(lo[:, sl], ones_bd)
        outs.append(acc)
    return jnp.concatenate(outs, axis=1)


def _swap_pair_halves(x, first_half):
    w = x.shape[1]
    return jnp.where(first_half, pltpu.roll(x, w - HEAD_DIM, axis=1), pltpu.roll(x, HEAD_DIM, axis=1))


def _rwkv_kernel(zm_ref, zl_ref, mum_ref, mul_ref, w0_ref, w2_ref, a0_ref, a2_ref, g2_ref,
                 kk_ref, ka_ref, rk_ref, lw_ref, lb_ref,
                 o_ref,
                 st_ref, carm_ref, carl_ref,
                 a_s, r_s, b_s, k_s, bh_s, kh_s, vsw_s, pe_s, y_s, y0_s,
                 *, tt, width, dlp, alp):
    c_len = RWKV_CHUNK
    n = HEAD_DIM
    pairs = width // LANES
    pp = RWKV_PAIRS_PER_STEP if pairs % RWKV_PAIRS_PER_STEP == 0 else 1

    @pl.when(pl.program_id(1) == 0)
    def _():
        st_ref[...] = jnp.zeros_like(st_ref)
        carm_ref[...] = jnp.zeros_like(carm_ref)
        carl_ref[...] = jnp.zeros_like(carl_ref)

    row = lax.broadcasted_iota(jnp.int32, (tt, 1), 0)

    def shift_lerp(x, car_ref, mu):
        xs = pltpu.roll(x, 1, axis=0)
        xs = jnp.where(row == 0, car_ref[...], xs)
        car_ref[...] = x[tt - 1:tt, :]
        return x + (xs - x) * mu

    zm = shift_lerp(zm_ref[...], carm_ref, mum_ref[...])
    zl = shift_lerp(zl_ref[...], carl_ref, mul_ref[...])
    r = zm[:, :width]
    k = zm[:, width:2 * width]
    v = zm[:, 2 * width:]
    xw = zl[:, :dlp]
    xa = zl[:, dlp:dlp + alp]
    xg = zl[:, dlp + alp:]

    wr = w0_ref[...] + _dot(_bf(jnp.tanh(xw)), w2_ref[...])
    softplus_neg = jnp.maximum(-wr, 0.0) + jnp.log(1.0 + jnp.exp(-jnp.abs(wr)))
    wlog = -jnp.exp(-softplus_neg - 0.5)
    a = _sigmoid(a0_ref[...] + _dot(_bf(xa), a2_ref[...]))
    g = _dot(_bf(_sigmoid(xg)), g2_ref[...])

    ri = lax.broadcasted_iota(jnp.int32, (MXU_DIM, MXU_DIM), 0)
    ci = lax.broadcasted_iota(jnp.int32, (MXU_DIM, MXU_DIM), 1)
    shift = int(math.log2(n))
    ones_bd = jnp.where((ri >> shift) == (ci >> shift), 1.0, 0.0).astype(BF16)

    kkv = k * kk_ref[...]
    ssq = _segsum(kkv * kkv, ones_bd, split=True)
    kkn = kkv / jnp.maximum(jnp.sqrt(ssq), 1e-12)
    k2 = k * (1.0 + (a - 1.0) * ka_ref[...])
    beta = kkn * a
    bonus = _segsum(r * k2 * rk_ref[...], ones_bd) * v

    rt_i = lax.broadcasted_iota(jnp.int32, (tt, tt), 0)
    ct_i = lax.broadcasted_iota(jnp.int32, (tt, tt), 1)
    cshift = int(math.log2(c_len))
    same = (rt_i >> cshift) == (ct_i >> cshift)
    lt_bd = jnp.where(same & (ct_i <= rt_i), 1.0, 0.0).astype(BF16)
    le_bd = jnp.where(same, 1.0, 0.0).astype(BF16)
    w_hi = _bf(wlog)
    w_lo = _bf(wlog - w_hi.astype(F32))
    cum = _dot(lt_bd, w_hi) + _dot(lt_bd, w_lo)
    cend = _dot(le_bd, w_hi) + _dot(le_bd, w_lo)

    first_half = (lax.broadcasted_iota(jnp.int32, (1, width), 1) & (LANES - 1)) < n
    inv = jnp.exp(-cum)
    dend = jnp.exp(cend - cum)
    a_s[...] = _bf(-kkn * jnp.exp(cum - wlog))
    r_s[...] = _bf(r * jnp.exp(cum))
    b_s[...] = _bf(beta * inv)
    k_s[...] = _bf(k2 * inv)
    bh_s[...] = _bf(beta * dend)
    kh_s[...] = _bf(k2 * dend)
    vsw_s[...] = _bf(_swap_pair_halves(v, first_half))
    pe_s[...] = jnp.exp(cend)

    lane = lax.broadcasted_iota(jnp.int32, (c_len, LANES), 1)
    rowi = lax.broadcasted_iota(jnp.int32, (c_len, LANES), 0)
    m0f = lane < n
    m0b = jnp.where(m0f, 1.0, 0.0).astype(BF16)
    m1b = jnp.where(m0f, 0.0, 1.0).astype(BF16)
    sidx = lane & (n - 1)
    strict2 = sidx < rowi
    incl2 = sidx <= rowi
    r2 = lax.broadcasted_iota(jnp.int32, (LANES, LANES), 0)
    l2 = lax.broadcasted_iota(jnp.int32, (LANES, LANES), 1)
    bd = (r2 < n) == (l2 < n)
    eye2 = r2 == l2
    zer = jnp.zeros((c_len, LANES), BF16)
    n_mid_rounds = int(math.log2(c_len)) - 2

    n_chunks = tt // c_len
    hpp = LANES // n

    def group_body(gidx, carry):
        lanes = [pl.ds(pl.multiple_of((gidx * pp + pi) * LANES, LANES), LANES) for pi in range(pp)]
        slabs = [(pi, c) for pi in range(pp) for c in range(n_chunks)]
        chains = [(si, q) for si in range(len(slabs)) for q in range(hpp)]
        rows = [slice(c * c_len, (c + 1) * c_len) for c in range(n_chunks)]

        a_p = [a_s[rows[c], lanes[pi]] for pi, c in slabs]
        r_p = [r_s[rows[c], lanes[pi]] for pi, c in slabs]
        bh_p = [bh_s[rows[c], lanes[pi]] for pi, c in slabs]
        kh_p = [kh_s[rows[c], lanes[pi]] for pi, c in slabs]
        vsw_p = [vsw_s[rows[c], lanes[pi]] for pi, c in slabs]
        bk = [jnp.concatenate([b_s[rows[c], lanes[pi]], k_s[rows[c], lanes[pi]]], axis=0) for pi, c in slabs]
        msk = [(m0b, m1b) if q == 0 else (m1b, m0b) for _, q in chains]

        aq = [a_p[si] * msk[ci][0] for ci, (si, q) in enumerate(chains)]
        vq = [vsw_p[si] * msk[ci][1] for ci, (si, q) in enumerate(chains)]
        sc = [_dot_nt(jnp.concatenate([aq[ci], r_p[si] * msk[ci][0]], axis=0), bk[si])
              for ci, (si, q) in enumerate(chains)]
        top = [_bf(jnp.where(strict2, x[:c_len], 0.0)) for x in sc]
        bot = [_bf(jnp.where(incl2, x[c_len:], 0.0)) for x in sc]
        z = [aq[ci].astype(F32) + _dot(top[ci], jnp.concatenate([zer, vq[ci]], axis=0))
             for ci in range(len(chains))]
        t = [_dot(top[ci][:, :n], jnp.concatenate([_bf(z[ci]), top[ci]], axis=1)) for ci in range(len(chains))]
        z = [z[ci] + t[ci][:, :LANES] for ci in range(len(chains))]
        pb = [_bf(t[ci][:, LANES:]) for ci in range(len(chains))]
        for _ in range(n_mid_rounds):
            t = [_dot(pb[ci][:, :n], jnp.concatenate([_bf(z[ci]), pb[ci]], axis=1)) for ci in range(len(chains))]
            z = [z[ci] + t[ci][:, :LANES] for ci in range(len(chains))]
            pb = [_bf(t[ci][:, LANES:]) for ci in range(len(chains))]
        z = [z[ci] + _dot(pb[ci][:, :n], _bf(z[ci])) for ci in range(len(chains))]
        qv = [jnp.concatenate([_bf(z[ci]), vq[ci]], axis=0) for ci in range(len(chains))]
        ws = [_dot(bot[ci], qv[ci]) for ci in range(len(chains))]
        gq = [_dot_tn(qv[ci], jnp.concatenate([bh_p[si] * msk[ci][0], kh_p[si] * msk[ci][0]], axis=0))
              for ci, (si, q) in enumerate(chains)]

        rhat, m_bd, n_bd = [], [], []
        for si, (pi, c) in enumerate(slabs):
            w0, w1 = ws[si * hpp], ws[si * hpp + 1]
            mn = gq[si * hpp] + gq[si * hpp + 1]
            rhat.append(_bf(r_p[si].astype(F32) + jnp.where(m0f, w0, w1)))
            y0_s[rows[c], lanes[pi]] = jnp.where(m0f, w1, w0)
            pe_p = pe_s[c * c_len:c * c_len + 1, lanes[pi]]
            m_bd.append(_bf(jnp.where(bd, mn, 0.0) + jnp.where(eye2, pe_p, 0.0)))
            n_sw = jnp.where(bd, 0.0, mn)
            n_bd.append(jnp.concatenate([n_sw[n:], n_sw[:n]], axis=0))

        s_bd = [st_ref[gidx * pp + pi] for pi in range(pp)]
        for c in range(n_chunks):
            sb = [_bf(s_bd[pi]) for pi in range(pp)]
            for pi in range(pp):
                y_s[rows[c], lanes[pi]] = _dot_nt(rhat[pi * n_chunks + c], sb[pi])
            s_bd = [_dot(sb[pi], m_bd[pi * n_chunks + c]) + n_bd[pi * n_chunks + c] for pi in range(pp)]
        for pi in range(pp):
            st_ref[gidx * pp + pi] = s_bd[pi]
        return carry

    lax.fori_loop(0, pairs // pp, group_body, 0)

    y = y_s[...] + _swap_pair_halves(y0_s[...], first_half)
    mean = _segsum(y, ones_bd) * (1.0 / n)
    yc = y - mean
    var = _segsum(yc * yc, ones_bd) * (1.0 / n)
    yn = yc * lax.rsqrt(var + LN_X_EPS) * lw_ref[...] + lb_ref[...]
    o_ref[...] = ((yn + bonus) * g).astype(o_ref.dtype)


def _rwkv(z, mu_m, mu_l, w0, w2p, a0, a2p, g2p, k_k, k_a, r_k, lnx_w, lnx_b, *, width, s5w, name):
    b, t, zc = z.shape
    dlp, alp, glp = w2p.shape[0], a2p.shape[0], g2p.shape[0]
    lw = dlp + alp + glp
    tt = _pick(t, (128, 64))
    lora_blk = (3 * width + s5w) // lw
    assert lora_blk * lw == 3 * width + s5w
    pairs = width // LANES

    def row(x):
        return x.reshape(1, -1).astype(F32)

    vec = pl.BlockSpec((1, width), lambda bi, i: (0, 0))
    kern = functools.partial(_rwkv_kernel, tt=tt, width=width, dlp=dlp, alp=alp)
    return pl.pallas_call(
        kern,
        grid=(b, t // tt),
        in_specs=[
            pl.BlockSpec((None, tt, 3 * width), lambda bi, i: (bi, i, 0)),
            pl.BlockSpec((None, tt, lw), lambda bi, i: (bi, i, lora_blk)),
            pl.BlockSpec((1, 3 * width), lambda bi, i: (0, 0)),
            pl.BlockSpec((1, lw), lambda bi, i: (0, 0)),
            vec,
            pl.BlockSpec((dlp, width), lambda bi, i: (0, 0)),
            vec,
            pl.BlockSpec((alp, width), lambda bi, i: (0, 0)),
            pl.BlockSpec((glp, width), lambda bi, i: (0, 0)),
            vec, vec, vec, vec, vec,
        ],
        out_specs=pl.BlockSpec((None, tt, width), lambda bi, i: (bi, i, 0)),
        out_shape=jax.ShapeDtypeStruct((b, t, width), BF16),
        scratch_shapes=[
            pltpu.VMEM((pairs, LANES, LANES), F32),
            pltpu.VMEM((1, 3 * width), F32),
            pltpu.VMEM((1, lw), F32),
        ] + [pltpu.VMEM((tt, width), BF16)] * 7 + [pltpu.VMEM((tt, width), F32)] * 3,
        compiler_params=_cparams(("parallel", "arbitrary")),
        name=name,
    )(z, z, row(mu_m), row(mu_l), row(w0), w2p, row(a0), a2p, g2p,
      row(k_k), row(k_a), row(r_k), row(lnx_w), row(lnx_b))


def _gelu_tanh(x):
    return x * (0.5 * (1.0 + jnp.tanh(0.7978845608028654 * (x + 0.044715 * (x * x * x)))))


def _s5_kernel(u_ref, bd_ref, wz_ref, gz_ref, m1_ref, m2_ref, o_ref, bdw_s, xp_s, *, nlev):
    lc, nc, _ = u_ref.shape
    gpb, ch, sp2 = wz_ref.shape[1:]
    half = sp2 // 2

    y_intra = []
    for t in range(lc):
        acc = _dot(u_ref[0], bd_ref[t])
        for s in range(1, t + 1):
            acc = acc + _dot(u_ref[s], bd_ref[t - s])
        y_intra.append(acc)

    bdw_s[...] = jnp.zeros_like(bdw_s)
    z = None
    for s in range(lc):
        buf = bdw_s.at[s % 2]
        for g in range(gpb):
            buf[g * ch:(g + 1) * ch, g * sp2:(g + 1) * sp2] = wz_ref[s, g]
        d = _dot(u_ref[s], buf[...])
        z = d if z is None else z + d

    cidx = lax.broadcasted_iota(jnp.int32, (nc, 1), 0)
    gs = range(gpb)
    x = [z[:, g * sp2:(g + 1) * sp2] for g in gs]
    for lev in range(nlev):
        sh = 1 << lev
        xs = [jnp.where(cidx >= sh, pltpu.roll(x[g], sh, axis=0), 0.0) for g in gs]
        xsw = [pltpu.roll(xs[g], half, axis=1) for g in gs]
        x = [x[g] + m1_ref[lev:lev + 1, g * sp2:(g + 1) * sp2] * xs[g]
             + m2_ref[lev:lev + 1, g * sp2:(g + 1) * sp2] * xsw[g] for g in gs]
    for g in gs:
        xp_s[:, g * sp2:(g + 1) * sp2] = _bf(jnp.where(cidx >= 1, pltpu.roll(x[g], 1, axis=0), 0.0))

    xp = xp_s[...]
    for t in range(lc):
        buf = bdw_s.at[t % 2]
        for g in range(gpb):
            buf[g * ch:(g + 1) * ch, g * sp2:(g + 1) * sp2] = gz_ref[t, g]
        y = y_intra[t] + _dot_nt(xp, buf[...])
        o_ref[t] = _gelu_tanh(y).astype(o_ref.dtype)


def _s5_core(u_p, bd, wz, gz, m1, m2, *, name):
    bsz, lc, nc, s5w = u_p.shape
    nblk, _, gpb, ch, sp2 = wz.shape
    nlev = m1.shape[1]
    return pl.pallas_call(
        functools.partial(_s5_kernel, nlev=nlev),
        grid=(bsz, nblk),
        in_specs=[
            pl.BlockSpec((None, lc, nc, MXU_DIM), lambda b, j: (b, 0, 0, j)),
            pl.BlockSpec((None, lc, MXU_DIM, MXU_DIM), lambda b, j: (j, 0, 0, 0)),
            pl.BlockSpec((None, lc, gpb, ch, sp2), lambda b, j: (j, 0, 0, 0, 0)),
            pl.BlockSpec((None, lc, gpb, ch, sp2), lambda b, j: (j, 0, 0, 0, 0)),
            pl.BlockSpec((None, nlev, gpb * sp2), lambda b, j: (j, 0, 0)),
            pl.BlockSpec((None, nlev, gpb * sp2), lambda b, j: (j, 0, 0)),
        ],
        out_specs=pl.BlockSpec((None, lc, nc, MXU_DIM), lambda b, j: (b, 0, 0, j)),
        out_shape=jax.ShapeDtypeStruct((bsz, lc, nc, s5w), BF16),
        scratch_shapes=[pltpu.VMEM((2, MXU_DIM, gpb * sp2), BF16), pltpu.VMEM((nc, gpb * sp2), BF16)],
        compiler_params=_cparams(("parallel", "arbitrary")),
        name=name,
    )(u_p, bd, wz, gz, m1, m2)


def _s5_operators(lam_re, lam_im, log_step, b_re, b_im, c_re, c_im, d_skip, *, nc):
    groups, sp = lam_re.shape
    ch = b_re.shape[2]
    lc = S5_CHUNK
    lr = lam_re.astype(F32)
    li = lam_im.astype(F32)
    dt = jnp.exp(log_step.astype(F32))[:, None]

    def apow(nsteps):
        nsteps = jnp.asarray(nsteps, F32)[:, None, None]
        mag = jnp.exp(lr * dt * nsteps)
        ang = li * dt * nsteps
        return mag * jnp.cos(ang), mag * jnp.sin(ang)

    ar, ai = apow(jnp.arange(lc + 1))
    den = lr * lr + li * li
    q_re = ((ar[1] - 1.0) * lr + ai[1] * li) / den
    q_im = (ai[1] * lr - (ar[1] - 1.0) * li) / den
    bb_re = q_re[:, :, None] * b_re - q_im[:, :, None] * b_im
    bb_im = q_re[:, :, None] * b_im + q_im[:, :, None] * b_re
    ce_re = c_re[None] * ar[:, :, None, :] - c_im[None] * ai[:, :, None, :]
    ce_im = c_re[None] * ai[:, :, None, :] + c_im[None] * ar[:, :, None, :]
    gpb = MXU_DIM // ch
    nblk = groups // gpb
    taps = jnp.sum(ce_re[:lc, :, None, :, :] * jnp.transpose(bb_re, (0, 2, 1))[None, :, :, None, :]
                   - ce_im[:lc, :, None, :, :] * jnp.transpose(bb_im, (0, 2, 1))[None, :, :, None, :], axis=-1)
    skip = jnp.eye(ch, dtype=F32)[None] * d_skip.reshape(groups, 1, ch).astype(F32)
    taps = taps.at[0].add(skip)
    taps = jnp.transpose(taps.reshape(lc, nblk, gpb, ch, ch), (1, 0, 2, 3, 4))
    same_group = jnp.eye(gpb, dtype=jnp.bool_)[None, None, :, None, :, None]
    bd = jnp.where(same_group, _bf(taps)[:, :, :, :, None, :], jnp.zeros((), BF16))
    bd = bd.reshape(nblk, lc, MXU_DIM, MXU_DIM)
    arr = ar[lc - 1 - jnp.arange(lc)]
    air = ai[lc - 1 - jnp.arange(lc)]
    w_re = arr[:, :, :, None] * bb_re[None] - air[:, :, :, None] * bb_im[None]
    w_im = arr[:, :, :, None] * bb_im[None] + air[:, :, :, None] * bb_re[None]
    wz = jnp.transpose(jnp.concatenate([w_re, w_im], axis=2), (1, 0, 3, 2))
    wz = jnp.transpose(wz.reshape(nblk, gpb, lc, ch, 2 * sp), (0, 2, 1, 3, 4))
    gz = jnp.transpose(jnp.concatenate([ce_re[1:], -ce_im[1:]], axis=3), (1, 0, 2, 3))
    gz = jnp.transpose(gz.reshape(nblk, gpb, lc, ch, 2 * sp), (0, 2, 1, 3, 4))
    nlev = max(1, int(math.ceil(math.log2(nc))))
    mr, mi = apow(lc * (2.0 ** jnp.arange(nlev)))

    def per_block(a):
        return jnp.transpose(a.reshape(nlev, nblk, gpb * 2 * sp), (1, 0, 2))

    m1 = per_block(jnp.concatenate([mr, mr], axis=2))
    m2 = per_block(jnp.concatenate([-mi, mi], axis=2))
    return bd, _bf(wz), _bf(gz), m1, m2


def _pad_rows(w, to):
    return jnp.pad(w, ((0, to - w.shape[0]), (0, 0)))


def _round_up(x, m):
    return (x + m - 1) // m * m


def kernel(x, p, g_mix_pre, w_in, mu, w0, w2, a0, a2, g2, k_k, k_a, r_k, lnx_w, lnx_b, lam_re, lam_im, log_step, b_re, b_im, c_re, c_im, d_skip, w_glu, b_glu, w_out, g_mix_post, g_ffn_pre, w_ff1, w_ff2, g_ffn_post, w_ple, g_ple_gate, w_ple_gate, g_ple_post):
    bsz, seq, d = x.shape
    width = w0.shape[-1]
    s5w = w_glu.shape[-1]
    dl, al, gl = w2.shape[1], a2.shape[1], g2.shape[1]
    dlp, alp, glp = _round_up(dl, LANES), _round_up(al, LANES), _round_up(gl, LANES)
    nc = seq // S5_CHUNK
    m = bsz * seq
    o_wl = 3 * width
    o_al = o_wl + dl
    o_gl = o_al + al
    o_s5 = o_gl + gl

    def pad_cols(wm, to):
        return jnp.pad(wm, ((0, 0), (0, to - wm.shape[1])))

    def layer(carry, lp):
        h, hn = carry
        (p_i, g_next_i, wi, mu_i, w0_i, w2_i, a0_i, a2_i, g2_i, k_k_i, k_a_i, r_k_i, lnx_w_i, lnx_b_i,
         lam_re_i, lam_im_i, log_step_i, b_re_i, b_im_i, c_re_i, c_im_i, d_skip_i, w_glu_i, b_glu_i, w_out_i,
         g_mix_post_i, g_ffn_pre_i, w_ff1_i, w_ff2_i, g_ffn_post_i, w_ple_i, g_ple_gate_i, w_ple_gate_i,
         g_ple_post_i) = lp
        w_perm = _bf(jnp.concatenate([
            wi[:, :o_wl], wi[:, o_s5:],
            pad_cols(wi[:, o_wl:o_al], dlp), pad_cols(wi[:, o_al:o_gl], alp), pad_cols(wi[:, o_gl:o_s5], glp)],
            axis=1))
        mu_m = mu_i[:o_wl]
        mu_l = jnp.concatenate([jnp.pad(mu_i[o_wl:o_al], (0, dlp - dl)), jnp.pad(mu_i[o_al:o_gl], (0, alp - al)),
                                jnp.pad(mu_i[o_gl:o_s5], (0, glp - gl))])

        z = _mm_rows(hn, w_perm, epilogue="none", out_dtype=F32, name="in_proj")
        z3 = z.reshape(bsz, seq, z.shape[1])
        y_rwkv = _rwkv(z3, mu_m, mu_l, w0_i, _bf(_pad_rows(w2_i, dlp)), a0_i, _bf(_pad_rows(a2_i, alp)),
                       _bf(_pad_rows(g2_i, glp)), k_k_i, k_a_i, r_k_i, lnx_w_i, lnx_b_i,
                       width=width, s5w=s5w, name="rwkv")

        u = _bf(z3[:, :, o_wl:o_wl + s5w])
        u_p = jnp.transpose(u.reshape(bsz, nc, S5_CHUNK, s5w), (0, 2, 1, 3))
        ops = _s5_operators(lam_re_i, lam_im_i, log_step_i, b_re_i, b_im_i, c_re_i, c_im_i, d_skip_i, nc=nc)
        y_p = _s5_core(u_p, *ops, name="s5")
        y_s = jnp.transpose(y_p, (0, 2, 1, 3)).reshape(m, s5w)
        y_s5 = _mm_rows(y_s, _bf(w_glu_i), epilogue="glu", out_dtype=BF16, extra=(y_s, b_glu_i), name="glu")

        if width == s5w:
            mix_in = [y_rwkv.reshape(m, width), y_s5]
        else:
            mix_in = [jnp.concatenate([y_rwkv.reshape(m, width), y_s5], axis=1)]
        mixed = _mm_kt(mix_in, _bf(w_out_i), name="out_proj")
        h, hn = _resid_norm(h, mixed, g_mix_post_i, g_ffn_pre_i, name="mix_post")

        hid = _mm_rows(hn, _bf(w_ff1_i), epilogue="relu2", out_dtype=BF16, name="ff1")
        f = _mm_kt([hid], _bf(w_ff2_i), name="ff2")
        h, hn = _resid_norm(h, f, g_ffn_post_i, g_ple_gate_i, name="ffn_post")

        ge = _mm_rows(hn, _bf(w_ple_gate_i), epilogue="ple", out_dtype=BF16,
                      extra=(_bf(p_i.reshape(m, -1)), _bf(w_ple_i)), name="ple")
        h, hn = _resid_norm(h, ge, g_ple_post_i, g_next_i, name="ple_post")
        return (h, hn), None

    g_next = jnp.roll(g_mix_pre, -1, axis=0)
    stacked = (p, g_next, w_in, mu, w0, w2, a0, a2, g2, k_k, k_a, r_k, lnx_w, lnx_b, lam_re, lam_im, log_step,
               b_re, b_im, c_re, c_im, d_skip, w_glu, b_glu, w_out, g_mix_post, g_ffn_pre, w_ff1, w_ff2, g_ffn_post,
               w_ple, g_ple_gate, w_ple_gate, g_ple_post)
    h0 = x.reshape(m, d)
    hn0 = _norm_cast(h0, g_mix_pre[0], name="pre_norm")
    carry = (h0, hn0)
    for i in range(w_in.shape[0]):
        carry, _ = layer(carry, tuple(a[i] for a in stacked))
    return carry[0].reshape(bsz, seq, d)
```

```python
import functools
import math

import jax
import jax.numpy as jnp
from jax import lax
from jax.experimental import pallas as pl
from jax.experimental.pallas import tpu as pltpu

F32 = jnp.float32
BF16 = jnp.bfloat16

V7X_VMEM_BYTES = 64 * 1024 * 1024
LANES = 128
MXU_DIM = 256

VMEM_LIMIT = V7X_VMEM_BYTES - 8 * 1024 * 1024

RMS_EPS = 1e-6
LN_X_EPS = 64e-5
HEAD_DIM = 64
S5_GROUP_CH = 16
S5_CHUNK = 16
RWKV_CHUNK = 64
RWKV_PAIRS_PER_STEP = 4


def _bf(x):
    return x.astype(BF16)


def _dot(a, b):
    return jnp.dot(a, b, preferred_element_type=F32)


def _dot_nt(a, b):
    return lax.dot_general(a, b, (((1,), (1,)), ((), ())), preferred_element_type=F32)


def _dot_tn(a, b):
    return lax.dot_general(a, b, (((0,), (0,)), ((), ())), preferred_element_type=F32)


def _sigmoid(x):
    return 1.0 / (1.0 + jnp.exp(-x))


def _cparams(sem):
    return pltpu.CompilerParams(dimension_semantics=sem, vmem_limit_bytes=VMEM_LIMIT)


def _pick(n, cands):
    for c in cands:
        if n % c == 0:
            return c
    return n


def _rms(x, g):
    ms = jnp.mean(x * x, axis=-1, keepdims=True)
    return (x * lax.rsqrt(ms + RMS_EPS)) * g


def _mm_rows_kernel(*refs, epilogue):
    if epilogue == "glu":
        a_ref, w_ref, y_ref, b_ref, o_ref = refs
    elif epilogue == "ple":
        a_ref, w_ref, p_ref, wp_ref, o_ref = refs
    else:
        a_ref, w_ref, o_ref = refs
    acc = _dot(a_ref[...], w_ref[...])
    if epilogue == "none":
        o_ref[...] = acc.astype(o_ref.dtype)
    elif epilogue == "relu2":
        r = jnp.maximum(acc, 0.0)
        o_ref[...] = (r * r).astype(o_ref.dtype)
    elif epilogue == "glu":
        o_ref[...] = (y_ref[...].astype(F32) * _sigmoid(acc + b_ref[...])).astype(o_ref.dtype)
    elif epilogue == "ple":
        e = _dot(p_ref[...], wp_ref[...])
        o_ref[...] = (_sigmoid(acc) * e).astype(o_ref.dtype)


def _mm_rows(a, w, *, epilogue, out_dtype, extra=(), name):
    m, k = a.shape
    n = w.shape[1]
    tm = _pick(m, (1024, 512, 256, 128))
    tn = _pick(n, (1024, 512, 256, 128))
    in_specs = [
        pl.BlockSpec((tm, k), lambda i, j: (i, 0)),
        pl.BlockSpec((k, tn), lambda i, j: (0, j)),
    ]
    args = [a, w]
    if epilogue == "glu":
        y, b = extra
        in_specs += [pl.BlockSpec((tm, tn), lambda i, j: (i, j)),
                     pl.BlockSpec((1, tn), lambda i, j: (0, j))]
        args += [y, b.reshape(1, n).astype(F32)]
    elif epilogue == "ple":
        p, wp = extra
        kp = p.shape[1]
        in_specs += [pl.BlockSpec((tm, kp), lambda i, j: (i, 0)),
                     pl.BlockSpec((kp, tn), lambda i, j: (0, j))]
        args += [p, wp]
    return pl.pallas_call(
        functools.partial(_mm_rows_kernel, epilogue=epilogue),
        grid=(m // tm, n // tn),
        in_specs=in_specs,
        out_specs=pl.BlockSpec((tm, tn), lambda i, j: (i, j)),
        out_shape=jax.ShapeDtypeStruct((m, n), out_dtype),
        compiler_params=_cparams(("parallel", "arbitrary")),
        name=name,
    )(*args)


def _mm_kt_kernel(*refs, slab, n_first):
    a_refs, (w_ref, o_ref, acc_ref) = refs[:-3], refs[-3:]
    n = o_ref.shape[1]
    kk = pl.program_id(1)

    def accumulate(a_ref, first):
        a = a_ref[...]
        for s in range(n // slab):
            sl = slice(s * slab, (s + 1) * slab)
            prod = _dot(a, w_ref[:, sl])
            if first:
                acc_ref[:, sl] = prod
            else:
                acc_ref[:, sl] += prod

    pl.when(kk == 0)(lambda: accumulate(a_refs[0], True))
    pl.when((kk > 0) & (kk < n_first))(lambda: accumulate(a_refs[0], False))
    if len(a_refs) > 1:
        pl.when(kk >= n_first)(lambda: accumulate(a_refs[1], False))

    @pl.when(kk == pl.num_programs(1) - 1)
    def _():
        for s in range(n // slab):
            sl = slice(s * slab, (s + 1) * slab)
            o_ref[:, sl] = acc_ref[:, sl].astype(o_ref.dtype)


def _mm_kt(a_list, w, *, name):
    m, k1 = a_list[0].shape
    assert all(a.shape == (m, k1) for a in a_list) and len(a_list) <= 2
    n = w.shape[1]
    tm = _pick(m, (1024, 512, 256, 128))
    tk = _pick(k1, (512, 256, 128))
    slab = _pick(n, (512, 256, 128))
    n_first = k1 // tk
    a_specs = [pl.BlockSpec((tm, tk), lambda i, kk: (i, jnp.minimum(kk, n_first - 1)))]
    if len(a_list) > 1:
        a_specs.append(pl.BlockSpec((tm, tk), lambda i, kk: (i, jnp.maximum(kk - n_first, 0))))
    return pl.pallas_call(
        functools.partial(_mm_kt_kernel, slab=slab, n_first=n_first),
        grid=(m // tm, len(a_list) * n_first),
        in_specs=a_specs + [pl.BlockSpec((tk, n), lambda i, kk: (kk, 0))],
        out_specs=pl.BlockSpec((tm, n), lambda i, kk: (i, 0)),
        out_shape=jax.ShapeDtypeStruct((m, n), BF16),
        scratch_shapes=[pltpu.VMEM((tm, n), F32)],
        compiler_params=_cparams(("parallel", "arbitrary")),
        name=name,
    )(*a_list, w)


def _norm_cast_kernel(x_ref, g_ref, o_ref):
    o_ref[...] = _bf(_rms(x_ref[...], g_ref[...]))


def _norm_cast(x, g, *, name):
    m, d = x.shape
    tm = _pick(m, (256, 128))
    spec = pl.BlockSpec((tm, d), lambda i: (i, 0))
    return pl.pallas_call(
        _norm_cast_kernel,
        grid=(m // tm,),
        in_specs=[spec, pl.BlockSpec((1, d), lambda i: (0, 0))],
        out_specs=spec,
        out_shape=jax.ShapeDtypeStruct((m, d), BF16),
        compiler_params=_cparams(("parallel",)),
        name=name,
    )(x, g.reshape(1, d).astype(F32))


def _resid_norm_kernel(h_ref, f_ref, gp_ref, gn_ref, o_ref, on_ref):
    h = h_ref[...] + _rms(f_ref[...].astype(F32), gp_ref[...])
    o_ref[...] = h
    on_ref[...] = _bf(_rms(h, gn_ref[...]))


def _resid_norm(h, f, g_post, g_next, *, name):
    m, d = h.shape
    tm = _pick(m, (256, 128))
    spec = pl.BlockSpec((tm, d), lambda i: (i, 0))
    vec = pl.BlockSpec((1, d), lambda i: (0, 0))
    return pl.pallas_call(
        _resid_norm_kernel,
        grid=(m // tm,),
        in_specs=[spec, spec, vec, vec],
        out_specs=[spec, spec],
        out_shape=[jax.ShapeDtypeStruct((m, d), F32), jax.ShapeDtypeStruct((m, d), BF16)],
        compiler_params=_cparams(("parallel",)),
        name=name,
    )(h, f, g_post.reshape(1, d).astype(F32), g_next.reshape(1, d).astype(F32))


def _segsum(x, ones_bd, split=False):
    hi = _bf(x)
    lo = _bf(x - hi.astype(F32)) if split else None
    outs = []
    for s in range(x.shape[1] // MXU_DIM):
        sl = slice(s * MXU_DIM, (s + 1) * MXU_DIM)
        acc = _dot(hi[:, sl], ones_bd)
        if split:
            acc = acc + _dot(lo[:, sl], ones_bd)
        outs.append(acc)
    return jnp.concatenate(outs, axis=1)


def _swap_pair_halves(x, first_half):
    w = x.shape[1]
    return jnp.where(first_half, pltpu.roll(x, w - HEAD_DIM, axis=1), pltpu.roll(x, HEAD_DIM, axis=1))


def _rwkv_kernel(zm_ref, zl_ref, mum_ref, mul_ref, w0_ref, w2_ref, a0_ref, a2_ref, g2_ref,
                 kk_ref, ka_ref, rk_ref, lw_ref, lb_ref,
                 o_ref,
                 st_ref, carm_ref, carl_ref,
                 a_s, r_s, b_s, k_s, bh_s, kh_s, vsw_s, pe_s, y_s, y0_s,
                 *, tt, width, dlp, alp):
    c_len = RWKV_CHUNK
    n = HEAD_DIM
    pairs = width // LANES
    pp = RWKV_PAIRS_PER_STEP if pairs % RWKV_PAIRS_PER_STEP == 0 else 1

    @pl.when(pl.program_id(1) == 0)
    def _():
        st_ref[...] = jnp.zeros_like(st_ref)
        carm_ref[...] = jnp.zeros_like(carm_ref)
        carl_ref[...] = jnp.zeros_like(carl_ref)

    row = lax.broadcasted_iota(jnp.int32, (tt, 1), 0)

    def shift_lerp(x, car_ref, mu):
        xs = pltpu.roll(x, 1, axis=0)
        xs = jnp.where(row == 0, car_ref[...], xs)
        car_ref[...] = x[tt - 1:tt, :]
        return x + (xs - x) * mu

    zm = shift_lerp(zm_ref[...], carm_ref, mum_ref[...])
    zl = shift_lerp(zl_ref[...], carl_ref, mul_ref[...])
    r = zm[:, :width]
    k = zm[:, width:2 * width]
    v = zm[:, 2 * width:]
    xw = zl[:, :dlp]
    xa = zl[:, dlp:dlp + alp]
    xg = zl[:, dlp + alp:]

    wr = w0_ref[...] + _dot(_bf(jnp.tanh(xw)), w2_ref[...])
    wlog = -math.exp(-0.5) * _sigmoid(wr)
    a = _sigmoid(a0_ref[...] + _dot(_bf(xa), a2_ref[...]))
    g = _dot(_bf(_sigmoid(xg)), g2_ref[...])

    ri = lax.broadcasted_iota(jnp.int32, (MXU_DIM, MXU_DIM), 0)
    ci = lax.broadcasted_iota(jnp.int32, (MXU_DIM, MXU_DIM), 1)
    shift = int(math.log2(n))
    ones_bd = jnp.where((ri >> shift) == (ci >> shift), 1.0, 0.0).astype(BF16)

    kkv = k * kk_ref[...]
    ssq = _segsum(kkv * kkv, ones_bd, split=True)
    kkn = kkv * lax.rsqrt(jnp.maximum(ssq, 1e-24))
    k2 = k * (1.0 + (a - 1.0) * ka_ref[...])
    beta = kkn * a
    bonus = _segsum(r * k2 * rk_ref[...], ones_bd) * v

    rt_i = lax.broadcasted_iota(jnp.int32, (tt, tt), 0)
    ct_i = lax.broadcasted_iota(jnp.int32, (tt, tt), 1)
    cshift = int(math.log2(c_len))
    same = (rt_i >> cshift) == (ct_i >> cshift)
    lt_bd = jnp.where(same & (ct_i <= rt_i), 1.0, 0.0).astype(BF16)
    le_bd = jnp.where(same, 1.0, 0.0).astype(BF16)
    w_hi = _bf(wlog)
    w_lo = _bf(wlog - w_hi.astype(F32))
    cum = _dot(lt_bd, w_hi) + _dot(lt_bd, w_lo)
    cend = _dot(le_bd, w_hi) + _dot(le_bd, w_lo)

    first_half = (lax.broadcasted_iota(jnp.int32, (1, width), 1) & (LANES - 1)) < n
    inv = jnp.exp(-cum)
    dend = jnp.exp(cend - cum)
    a_s[...] = _bf(-kkn * jnp.exp(cum - wlog))
    r_s[...] = _bf(r * jnp.exp(cum))
    b_s[...] = _bf(beta * inv)
    k_s[...] = _bf(k2 * inv)
    bh_s[...] = _bf(beta * dend)
    kh_s[...] = _bf(k2 * dend)
    vsw_s[...] = _bf(_swap_pair_halves(v, first_half))
    pe_s[...] = jnp.exp(cend)

    lane = lax.broadcasted_iota(jnp.int32, (c_len, LANES), 1)
    rowi = lax.broadcasted_iota(jnp.int32, (c_len, LANES), 0)
    m0f = lane < n
    m0b = jnp.where(m0f, 1.0, 0.0).astype(BF16)
    m1b = jnp.where(m0f, 0.0, 1.0).astype(BF16)
    sidx = lane & (n - 1)
    strict2 = sidx < rowi
    incl2 = sidx <= rowi
    r2 = lax.broadcasted_iota(jnp.int32, (LANES, LANES), 0)
    l2 = lax.broadcasted_iota(jnp.int32, (LANES, LANES), 1)
    bd = (r2 < n) == (l2 < n)
    eye2 = r2 == l2
    zer = jnp.zeros((c_len, LANES), BF16)
    n_mid_rounds = int(math.log2(c_len)) - 2

    n_chunks = tt // c_len
    hpp = LANES // n

    def group_body(gidx, carry):
        lanes = [pl.ds(pl.multiple_of((gidx * pp + pi) * LANES, LANES), LANES) for pi in range(pp)]
        slabs = [(pi, c) for pi in range(pp) for c in range(n_chunks)]
        chains = [(si, q) for si in range(len(slabs)) for q in range(hpp)]
        rows = [slice(c * c_len, (c + 1) * c_len) for c in range(n_chunks)]

        a_p = [a_s[rows[c], lanes[pi]] for pi, c in slabs]
        r_p = [r_s[rows[c], lanes[pi]] for pi, c in slabs]
        bh_p = [bh_s[rows[c], lanes[pi]] for pi, c in slabs]
        kh_p = [kh_s[rows[c], lanes[pi]] for pi, c in slabs]
        vsw_p = [vsw_s[rows[c], lanes[pi]] for pi, c in slabs]
        bk = [jnp.concatenate([b_s[rows[c], lanes[pi]], k_s[rows[c], lanes[pi]]], axis=0) for pi, c in slabs]
        msk = [(m0b, m1b) if q == 0 else (m1b, m0b) for _, q in chains]

        aq = [a_p[si] * msk[ci][0] for ci, (si, q) in enumerate(chains)]
        vq = [vsw_p[si] * msk[ci][1] for ci, (si, q) in enumerate(chains)]
        sc = [_dot_nt(jnp.concatenate([aq[ci], r_p[si] * msk[ci][0]], axis=0), bk[si])
              for ci, (si, q) in enumerate(chains)]
        top = [_bf(jnp.where(strict2, x[:c_len], 0.0)) for x in sc]
        bot = [_bf(jnp.where(incl2, x[c_len:], 0.0)) for x in sc]
        z = [aq[ci].astype(F32) + _dot(top[ci], jnp.concatenate([zer, vq[ci]], axis=0))
             for ci in range(len(chains))]
        t = [_dot(top[ci][:, :n], jnp.concatenate([_bf(z[ci]), top[ci]], axis=1)) for ci in range(len(chains))]
        z = [z[ci] + t[ci][:, :LANES] for ci in range(len(chains))]
        pb = [_bf(t[ci][:, LANES:]) for ci in range(len(chains))]
        for _ in range(n_mid_rounds):
            t = [_dot(pb[ci][:, :n], jnp.concatenate([_bf(z[ci]), pb[ci]], axis=1)) for ci in range(len(chains))]
            z = [z[ci] + t[ci][:, :LANES] for ci in range(len(chains))]
            pb = [_bf(t[ci][:, LANES:]) for ci in range(len(chains))]
        z = [z[ci] + _dot(pb[ci][:, :n], _bf(z[ci])) for ci in range(len(chains))]
        qv = [jnp.concatenate([_bf(z[ci]), vq[ci]], axis=0) for ci in range(len(chains))]
        ws = [_dot(bot[ci], qv[ci]) for ci in range(len(chains))]
        gq = [_dot_tn(qv[ci], jnp.concatenate([bh_p[si] * msk[ci][0], kh_p[si] * msk[ci][0]], axis=0))
              for ci, (si, q) in enumerate(chains)]

        rhat, m_bd, n_bd = [], [], []
        for si, (pi, c) in enumerate(slabs):
            w0, w1 = ws[si * hpp], ws[si * hpp + 1]
            mn = gq[si * hpp] + gq[si * hpp + 1]
            rhat.append(_bf(r_p[si].astype(F32) + jnp.where(m0f, w0, w1)))
            y0_s[rows[c], lanes[pi]] = jnp.where(m0f, w1, w0)
            pe_p = pe_s[c * c_len:c * c_len + 1, lanes[pi]]
            m_bd.append(_bf(jnp.where(bd, mn, 0.0) + jnp.where(eye2, pe_p, 0.0)))
            n_sw = jnp.where(bd, 0.0, mn)
            n_bd.append(jnp.concatenate([n_sw[n:], n_sw[:n]], axis=0))

        s_bd = [st_ref[gidx * pp + pi] for pi in range(pp)]
        for c in range(n_chunks):
            sb = [_bf(s_bd[pi]) for pi in range(pp)]
            for pi in range(pp):
                y_s[rows[c], lanes[pi]] = _dot_nt(rhat[pi * n_chunks + c], sb[pi])
            s_bd = [_dot(sb[pi], m_bd[pi * n_chunks + c]) + n_bd[pi * n_chunks + c] for pi in range(pp)]
        for pi in range(pp):
            st_ref[gidx * pp + pi] = s_bd[pi]
        return carry

    lax.fori_loop(0, pairs // pp, group_body, 0)

    y = y_s[...] + _swap_pair_halves(y0_s[...], first_half)
    mean = _segsum(y, ones_bd) * (1.0 / n)
    yc = y - mean
    var = _segsum(yc * yc, ones_bd) * (1.0 / n)
    yn = yc * lax.rsqrt(var + LN_X_EPS) * lw_ref[...] + lb_ref[...]
    o_ref[...] = ((yn + bonus) * g).astype(o_ref.dtype)


def _rwkv(z, mu_m, mu_l, w0, w2p, a0, a2p, g2p, k_k, k_a, r_k, lnx_w, lnx_b, *, width, s5w, name):
    b, t, zc = z.shape
    dlp, alp, glp = w2p.shape[0], a2p.shape[0], g2p.shape[0]
    lw = dlp + alp + glp
    tt = _pick(t, (128, 64))
    lora_blk = (3 * width + s5w) // lw
    assert lora_blk * lw == 3 * width + s5w
    pairs = width // LANES

    def row(x):
        return x.reshape(1, -1).astype(F32)

    vec = pl.BlockSpec((1, width), lambda bi, i: (0, 0))
    kern = functools.partial(_rwkv_kernel, tt=tt, width=width, dlp=dlp, alp=alp)
    return pl.pallas_call(
        kern,
        grid=(b, t // tt),
        in_specs=[
            pl.BlockSpec((None, tt, 3 * width), lambda bi, i: (bi, i, 0)),
            pl.BlockSpec((None, tt, lw), lambda bi, i: (bi, i, lora_blk)),
            pl.BlockSpec((1, 3 * width), lambda bi, i: (0, 0)),
            pl.BlockSpec((1, lw), lambda bi, i: (0, 0)),
            vec,
            pl.BlockSpec((dlp, width), lambda bi, i: (0, 0)),
            vec,
            pl.BlockSpec((alp, width), lambda bi, i: (0, 0)),
            pl.BlockSpec((glp, width), lambda bi, i: (0, 0)),
            vec, vec, vec, vec, vec,
        ],
        out_specs=pl.BlockSpec((None, tt, width), lambda bi, i: (bi, i, 0)),
        out_shape=jax.ShapeDtypeStruct((b, t, width), BF16),
        scratch_shapes=[
            pltpu.VMEM((pairs, LANES, LANES), F32),
            pltpu.VMEM((1, 3 * width), F32),
            pltpu.VMEM((1, lw), F32),
        ] + [pltpu.VMEM((tt, width), BF16)] * 7 + [pltpu.VMEM((tt, width), F32)] * 3,
        compiler_params=_cparams(("parallel", "arbitrary")),
        name=name,
    )(z, z, row(mu_m), row(mu_l), row(w0), w2p, row(a0), a2p, g2p,
      row(k_k), row(k_a), row(r_k), row(lnx_w), row(lnx_b))


def _gelu_tanh(x):
    return x * (0.5 * (1.0 + jnp.tanh(0.7978845608028654 * (x + 0.044715 * (x * x * x)))))


def _s5_kernel(u_ref, bd_ref, wz_ref, gz_ref, m1_ref, m2_ref, o_ref, bdw_s, xp_s, *, nlev):
    lc, nc, _ = u_ref.shape
    gpb, ch, sp2 = wz_ref.shape[1:]
    half = sp2 // 2

    bdw_s[...] = jnp.zeros_like(bdw_s)
    z = None
    for s in range(lc):
        buf = bdw_s.at[s % 2]
        for g in range(gpb):
            buf[g * ch:(g + 1) * ch, g * sp2:(g + 1) * sp2] = wz_ref[s, g]
        d = _dot(u_ref[s], buf[...])
        z = d if z is None else z + d

    cidx = lax.broadcasted_iota(jnp.int32, (nc, 1), 0)
    gs = range(gpb)
    x = [z[:, g * sp2:(g + 1) * sp2] for g in gs]

    def scan_level(x, lev):
        sh = 1 << lev
        xs = [jnp.where(cidx >= sh, pltpu.roll(x[g], sh, axis=0), 0.0) for g in gs]
        xsw = [pltpu.roll(xs[g], half, axis=1) for g in gs]
        return [x[g] + m1_ref[lev:lev + 1, g * sp2:(g + 1) * sp2] * xs[g]
                + m2_ref[lev:lev + 1, g * sp2:(g + 1) * sp2] * xsw[g] for g in gs]

    y_intra = [None] * lc
    for i, t in enumerate(reversed(range(lc))):
        acc = _dot(u_ref[0], bd_ref[t])
        for s in range(1, t + 1):
            acc = acc + _dot(u_ref[s], bd_ref[t - s])
        y_intra[t] = acc
        if i < nlev:
            x = scan_level(x, i)
    for lev in range(lc, nlev):
        x = scan_level(x, lev)
    for g in gs:
        xp_s[:, g * sp2:(g + 1) * sp2] = _bf(jnp.where(cidx >= 1, pltpu.roll(x[g], 1, axis=0), 0.0))

    xp = xp_s[...]
    for t in range(lc):
        buf = bdw_s.at[t % 2]
        for g in range(gpb):
            buf[g * ch:(g + 1) * ch, g * sp2:(g + 1) * sp2] = gz_ref[t, g]
        y = y_intra[t] + _dot_nt(xp, buf[...])
        o_ref[t] = _gelu_tanh(y).astype(o_ref.dtype)


def _s5_core(u_p, bd, wz, gz, m1, m2, *, name):
    bsz, lc, nc, s5w = u_p.shape
    nblk, _, gpb, ch, sp2 = wz.shape
    nlev = m1.shape[1]
    return pl.pallas_call(
        functools.partial(_s5_kernel, nlev=nlev),
        grid=(bsz, nblk),
        in_specs=[
            pl.BlockSpec((None, lc, nc, MXU_DIM), lambda b, j: (b, 0, 0, j)),
            pl.BlockSpec((None, lc, MXU_DIM, MXU_DIM), lambda b, j: (j, 0, 0, 0)),
            pl.BlockSpec((None, lc, gpb, ch, sp2), lambda b, j: (j, 0, 0, 0, 0)),
            pl.BlockSpec((None, lc, gpb, ch, sp2), lambda b, j: (j, 0, 0, 0, 0)),
            pl.BlockSpec((None, nlev, gpb * sp2), lambda b, j: (j, 0, 0)),
            pl.BlockSpec((None, nlev, gpb * sp2), lambda b, j: (j, 0, 0)),
        ],
        out_specs=pl.BlockSpec((None, lc, nc, MXU_DIM), lambda b, j: (b, 0, 0, j)),
        out_shape=jax.ShapeDtypeStruct((bsz, lc, nc, s5w), BF16),
        scratch_shapes=[pltpu.VMEM((2, MXU_DIM, gpb * sp2), BF16), pltpu.VMEM((nc, gpb * sp2), BF16)],
        compiler_params=_cparams(("parallel", "arbitrary")),
        name=name,
    )(u_p, bd, wz, gz, m1, m2)


def _s5_operators(lam_re, lam_im, log_step, b_re, b_im, c_re, c_im, d_skip, *, nc):
    groups, sp = lam_re.shape
    ch = b_re.shape[2]
    lc = S5_CHUNK
    lr = lam_re.astype(F32)
    li = lam_im.astype(F32)
    dt = jnp.exp(log_step.astype(F32))[:, None]

    def apow(nsteps):
        nsteps = jnp.asarray(nsteps, F32)[:, None, None]
        mag = jnp.exp(lr * dt * nsteps)
        ang = li * dt * nsteps
        return mag * jnp.cos(ang), mag * jnp.sin(ang)

    ar, ai = apow(jnp.arange(lc + 1))
    den = lr * lr + li * li
    q_re = ((ar[1] - 1.0) * lr + ai[1] * li) / den
    q_im = (ai[1] * lr - (ar[1] - 1.0) * li) / den
    bb_re = q_re[:, :, None] * b_re - q_im[:, :, None] * b_im
    bb_im = q_re[:, :, None] * b_im + q_im[:, :, None] * b_re
    ce_re = c_re[None] * ar[:, :, None, :] - c_im[None] * ai[:, :, None, :]
    ce_im = c_re[None] * ai[:, :, None, :] + c_im[None] * ar[:, :, None, :]
    gpb = MXU_DIM // ch
    nblk = groups // gpb
    taps = jnp.sum(ce_re[:lc, :, None, :, :] * jnp.transpose(bb_re, (0, 2, 1))[None, :, :, None, :]
                   - ce_im[:lc, :, None, :, :] * jnp.transpose(bb_im, (0, 2, 1))[None, :, :, None, :], axis=-1)
    skip = jnp.eye(ch, dtype=F32)[None] * d_skip.reshape(groups, 1, ch).astype(F32)
    taps = taps.at[0].add(skip)
    taps = jnp.transpose(taps.reshape(lc, nblk, gpb, ch, ch), (1, 0, 2, 3, 4))
    same_group = jnp.eye(gpb, dtype=jnp.bool_)[None, None, :, None, :, None]
    bd = jnp.where(same_group, _bf(taps)[:, :, :, :, None, :], jnp.zeros((), BF16))
    bd = bd.reshape(nblk, lc, MXU_DIM, MXU_DIM)
    arr = ar[lc - 1 - jnp.arange(lc)]
    air = ai[lc - 1 - jnp.arange(lc)]
    w_re = arr[:, :, :, None] * bb_re[None] - air[:, :, :, None] * bb_im[None]
    w_im = arr[:, :, :, None] * bb_im[None] + air[:, :, :, None] * bb_re[None]
    wz = jnp.transpose(jnp.concatenate([w_re, w_im], axis=2), (1, 0, 3, 2))
    wz = jnp.transpose(wz.reshape(nblk, gpb, lc, ch, 2 * sp), (0, 2, 1, 3, 4))
    gz = jnp.transpose(jnp.concatenate([ce_re[1:], -ce_im[1:]], axis=3), (1, 0, 2, 3))
    gz = jnp.transpose(gz.reshape(nblk, gpb, lc, ch, 2 * sp), (0, 2, 1, 3, 4))
    nlev = max(1, int(math.ceil(math.log2(nc))))
    mr, mi = apow(lc * (2.0 ** jnp.arange(nlev)))

    def per_block(a):
        return jnp.transpose(a.reshape(nlev, nblk, gpb * 2 * sp), (1, 0, 2))

    m1 = per_block(jnp.concatenate([mr, mr], axis=2))
    m2 = per_block(jnp.concatenate([-mi, mi], axis=2))
    return bd, _bf(wz), _bf(gz), m1, m2


def _pad_rows(w, to):
    return jnp.pad(w, ((0, to - w.shape[0]), (0, 0)))


def _round_up(x, m):
    return (x + m - 1) // m * m


def kernel(x, p, g_mix_pre, w_in, mu, w0, w2, a0, a2, g2, k_k, k_a, r_k, lnx_w, lnx_b, lam_re, lam_im, log_step, b_re, b_im, c_re, c_im, d_skip, w_glu, b_glu, w_out, g_mix_post, g_ffn_pre, w_ff1, w_ff2, g_ffn_post, w_ple, g_ple_gate, w_ple_gate, g_ple_post):
    bsz, seq, d = x.shape
    width = w0.shape[-1]
    s5w = w_glu.shape[-1]
    dl, al, gl = w2.shape[1], a2.shape[1], g2.shape[1]
    dlp, alp, glp = _round_up(dl, LANES), _round_up(al, LANES), _round_up(gl, LANES)
    nc = seq // S5_CHUNK
    m = bsz * seq
    o_wl = 3 * width
    o_al = o_wl + dl
    o_gl = o_al + al
    o_s5 = o_gl + gl

    def pad_cols(wm, to):
        return jnp.pad(wm, ((0, 0), (0, to - wm.shape[1])))

    def layer(carry, lp):
        h, hn = carry
        (p_i, g_next_i, wi, mu_i, w0_i, w2_i, a0_i, a2_i, g2_i, k_k_i, k_a_i, r_k_i, lnx_w_i, lnx_b_i,
         lam_re_i, lam_im_i, log_step_i, b_re_i, b_im_i, c_re_i, c_im_i, d_skip_i, w_glu_i, b_glu_i, w_out_i,
         g_mix_post_i, g_ffn_pre_i, w_ff1_i, w_ff2_i, g_ffn_post_i, w_ple_i, g_ple_gate_i, w_ple_gate_i,
         g_ple_post_i) = lp
        w_perm = _bf(jnp.concatenate([
            wi[:, :o_wl], wi[:, o_s5:],
            pad_cols(wi[:, o_wl:o_al], dlp), pad_cols(wi[:, o_al:o_gl], alp), pad_cols(wi[:, o_gl:o_s5], glp)],
            axis=1))
        mu_m = mu_i[:o_wl]
        mu_l = jnp.concatenate([jnp.pad(mu_i[o_wl:o_al], (0, dlp - dl)), jnp.pad(mu_i[o_al:o_gl], (0, alp - al)),
                                jnp.pad(mu_i[o_gl:o_s5], (0, glp - gl))])

        z = _mm_rows(hn, w_perm, epilogue="none", out_dtype=F32, name="in_proj")
        z3 = z.reshape(bsz, seq, z.shape[1])
        y_rwkv = _rwkv(z3, mu_m, mu_l, w0_i, _bf(_pad_rows(w2_i, dlp)), a0_i, _bf(_pad_rows(a2_i, alp)),
                       _bf(_pad_rows(g2_i, glp)), k_k_i, k_a_i, r_k_i, lnx_w_i, lnx_b_i,
                       width=width, s5w=s5w, name="rwkv")

        u = _bf(z3[:, :, o_wl:o_wl + s5w])
        u_p = jnp.transpose(u.reshape(bsz, nc, S5_CHUNK, s5w), (0, 2, 1, 3))
        ops = _s5_operators(lam_re_i, lam_im_i, log_step_i, b_re_i, b_im_i, c_re_i, c_im_i, d_skip_i, nc=nc)
        y_p = _s5_core(u_p, *ops, name="s5")
        y_s = jnp.transpose(y_p, (0, 2, 1, 3)).reshape(m, s5w)
        y_s5 = _mm_rows(y_s, _bf(w_glu_i), epilogue="glu", out_dtype=BF16, extra=(y_s, b_glu_i), name="glu")

        if width == s5w:
            mix_in = [y_rwkv.reshape(m, width), y_s5]
        else:
            mix_in = [jnp.concatenate([y_rwkv.reshape(m, width), y_s5], axis=1)]
        mixed = _mm_kt(mix_in, _bf(w_out_i), name="out_proj")
        h, hn = _resid_norm(h, mixed, g_mix_post_i, g_ffn_pre_i, name="mix_post")

        hid = _mm_rows(hn, _bf(w_ff1_i), epilogue="relu2", out_dtype=BF16, name="ff1")
        f = _mm_kt([hid], _bf(w_ff2_i), name="ff2")
        h, hn = _resid_norm(h, f, g_ffn_post_i, g_ple_gate_i, name="ffn_post")

        ge = _mm_rows(hn, _bf(w_ple_gate_i), epilogue="ple", out_dtype=BF16,
                      extra=(_bf(p_i.reshape(m, -1)), _bf(w_ple_i)), name="ple")
        h, hn = _resid_norm(h, ge, g_ple_post_i, g_next_i, name="ple_post")
        return (h, hn), None

    g_next = jnp.roll(g_mix_pre, -1, axis=0)
    stacked = (p, g_next, w_in, mu, w0, w2, a0, a2, g2, k_k, k_a, r_k, lnx_w, lnx_b, lam_re, lam_im, log_step,
               b_re, b_im, c_re, c_im, d_skip, w_glu, b_glu, w_out, g_mix_post, g_ffn_pre, w_ff1, w_ff2, g_ffn_post,
               w_ple, g_ple_gate, w_ple_gate, g_ple_post)
    h0 = x.reshape(m, d)
    hn0 = _norm_cast(h0, g_mix_pre[0], name="pre_norm")
    carry = (h0, hn0)
    for i in range(w_in.shape[0]):
        carry, _ = layer(carry, tuple(a[i] for a in stacked))
    return carry[0].reshape(bsz, seq, d)
```

```python
import functools
import math

import jax
import jax.numpy as jnp
from jax import lax
from jax.experimental import pallas as pl
from jax.experimental.pallas import tpu as pltpu

F32 = jnp.float32
BF16 = jnp.bfloat16

V7X_VMEM_BYTES = 64 * 1024 * 1024
LANES = 128
MXU_DIM = 256

VMEM_LIMIT = V7X_VMEM_BYTES - 8 * 1024 * 1024

RMS_EPS = 1e-6
LN_X_EPS = 64e-5
HEAD_DIM = 64
S5_GROUP_CH = 16
S5_CHUNK = 16
RWKV_CHUNK = 64
RWKV_PAIRS_PER_STEP = 8


def _bf(x):
    return x.astype(BF16)


def _dot(a, b):
    return jnp.dot(a, b, preferred_element_type=F32)


def _dot_nt(a, b):
    return lax.dot_general(a, b, (((1,), (1,)), ((), ())), preferred_element_type=F32)


def _dot_tn(a, b):
    return lax.dot_general(a, b, (((0,), (0,)), ((), ())), preferred_element_type=F32)


def _sigmoid(x):
    return 1.0 / (1.0 + jnp.exp(-x))


def _cparams(sem):
    return pltpu.CompilerParams(dimension_semantics=sem, vmem_limit_bytes=VMEM_LIMIT)


def _pick(n, cands):
    for c in cands:
        if n % c == 0:
            return c
    return n


def _rms(x, g):
    ms = jnp.mean(x * x, axis=-1, keepdims=True)
    return (x * lax.rsqrt(ms + RMS_EPS)) * g


def _mm_rows_kernel(*refs, epilogue):
    if epilogue == "glu":
        a_ref, w_ref, y_ref, b_ref, o_ref = refs
    elif epilogue == "ple":
        a_ref, w_ref, p_ref, wp_ref, o_ref = refs
    else:
        a_ref, w_ref, o_ref = refs
    acc = _dot(a_ref[...], w_ref[...])
    if epilogue == "none":
        o_ref[...] = acc.astype(o_ref.dtype)
    elif epilogue == "relu2":
        r = jnp.maximum(acc, 0.0)
        o_ref[...] = (r * r).astype(o_ref.dtype)
    elif epilogue == "glu":
        o_ref[...] = (y_ref[...].astype(F32) * _sigmoid(acc + b_ref[...])).astype(o_ref.dtype)
    elif epilogue == "ple":
        e = _dot(p_ref[...], wp_ref[...])
        o_ref[...] = (_sigmoid(acc) * e).astype(o_ref.dtype)


def _mm_rows(a, w, *, epilogue, out_dtype, extra=(), name):
    m, k = a.shape
    n = w.shape[1]
    tm = _pick(m, (1024, 512, 256, 128))
    tn = _pick(n, (1024, 512, 256, 128))
    in_specs = [
        pl.BlockSpec((tm, k), lambda i, j: (i, 0)),
        pl.BlockSpec((k, tn), lambda i, j: (0, j)),
    ]
    args = [a, w]
    if epilogue == "glu":
        y, b = extra
        in_specs += [pl.BlockSpec((tm, tn), lambda i, j: (i, j)),
                     pl.BlockSpec((1, tn), lambda i, j: (0, j))]
        args += [y, b.reshape(1, n).astype(F32)]
    elif epilogue == "ple":
        p, wp = extra
        kp = p.shape[1]
        in_specs += [pl.BlockSpec((tm, kp), lambda i, j: (i, 0)),
                     pl.BlockSpec((kp, tn), lambda i, j: (0, j))]
        args += [p, wp]
    return pl.pallas_call(
        functools.partial(_mm_rows_kernel, epilogue=epilogue),
        grid=(m // tm, n // tn),
        in_specs=in_specs,
        out_specs=pl.BlockSpec((tm, tn), lambda i, j: (i, j)),
        out_shape=jax.ShapeDtypeStruct((m, n), out_dtype),
        compiler_params=_cparams(("parallel", "arbitrary")),
        name=name,
    )(*args)


def _mm_kt_kernel(*refs, slab, n_first):
    a_refs, (w_ref, o_ref, acc_ref) = refs[:-3], refs[-3:]
    n = o_ref.shape[1]
    kk = pl.program_id(1)

    def accumulate(a_ref, first):
        a = a_ref[...]
        for s in range(n // slab):
            sl = slice(s * slab, (s + 1) * slab)
            prod = _dot(a, w_ref[:, sl])
            if first:
                acc_ref[:, sl] = prod
            else:
                acc_ref[:, sl] += prod

    pl.when(kk == 0)(lambda: accumulate(a_refs[0], True))
    pl.when((kk > 0) & (kk < n_first))(lambda: accumulate(a_refs[0], False))
    if len(a_refs) > 1:
        pl.when(kk >= n_first)(lambda: accumulate(a_refs[1], False))

    @pl.when(kk == pl.num_programs(1) - 1)
    def _():
        for s in range(n // slab):
            sl = slice(s * slab, (s + 1) * slab)
            o_ref[:, sl] = acc_ref[:, sl].astype(o_ref.dtype)


def _mm_kt(a_list, w, *, name):
    m, k1 = a_list[0].shape
    assert all(a.shape == (m, k1) for a in a_list) and len(a_list) <= 2
    n = w.shape[1]
    tm = _pick(m, (1024, 512, 256, 128))
    tk = _pick(k1, (512, 256, 128))
    slab = _pick(n, (512, 256, 128))
    n_first = k1 // tk
    a_specs = [pl.BlockSpec((tm, tk), lambda i, kk: (i, jnp.minimum(kk, n_first - 1)))]
    if len(a_list) > 1:
        a_specs.append(pl.BlockSpec((tm, tk), lambda i, kk: (i, jnp.maximum(kk - n_first, 0))))
    return pl.pallas_call(
        functools.partial(_mm_kt_kernel, slab=slab, n_first=n_first),
        grid=(m // tm, len(a_list) * n_first),
        in_specs=a_specs + [pl.BlockSpec((tk, n), lambda i, kk: (kk, 0))],
        out_specs=pl.BlockSpec((tm, n), lambda i, kk: (i, 0)),
        out_shape=jax.ShapeDtypeStruct((m, n), BF16),
        scratch_shapes=[pltpu.VMEM((tm, n), F32)],
        compiler_params=_cparams(("parallel", "arbitrary")),
        name=name,
    )(*a_list, w)


def _norm_cast_kernel(x_ref, g_ref, o_ref):
    o_ref[...] = _bf(_rms(x_ref[...], g_ref[...]))


def _norm_cast(x, g, *, name):
    m, d = x.shape
    tm = _pick(m, (256, 128))
    spec = pl.BlockSpec((tm, d), lambda i: (i, 0))
    return pl.pallas_call(
        _norm_cast_kernel,
        grid=(m // tm,),
        in_specs=[spec, pl.BlockSpec((1, d), lambda i: (0, 0))],
        out_specs=spec,
        out_shape=jax.ShapeDtypeStruct((m, d), BF16),
        compiler_params=_cparams(("parallel",)),
        name=name,
    )(x, g.reshape(1, d).astype(F32))


def _resid_norm_kernel(h_ref, f_ref, gp_ref, gn_ref, o_ref, on_ref):
    h = h_ref[...] + _rms(f_ref[...].astype(F32), gp_ref[...])
    o_ref[...] = h
    on_ref[...] = _bf(_rms(h, gn_ref[...]))


def _resid_norm(h, f, g_post, g_next, *, name):
    m, d = h.shape
    tm = _pick(m, (256, 128))
    spec = pl.BlockSpec((tm, d), lambda i: (i, 0))
    vec = pl.BlockSpec((1, d), lambda i: (0, 0))
    return pl.pallas_call(
        _resid_norm_kernel,
        grid=(m // tm,),
        in_specs=[spec, spec, vec, vec],
        out_specs=[spec, spec],
        out_shape=[jax.ShapeDtypeStruct((m, d), F32), jax.ShapeDtypeStruct((m, d), BF16)],
        compiler_params=_cparams(("parallel",)),
        name=name,
    )(h, f, g_post.reshape(1, d).astype(F32), g_next.reshape(1, d).astype(F32))


def _segsum(x, ones_bd, split=False):
    hi = _bf(x)
    lo = _bf(x - hi.astype(F32)) if split else None
    outs = []
    for s in range(x.shape[1] // MXU_DIM):
        sl = slice(s * MXU_DIM, (s + 1) * MXU_DIM)
        acc = _dot(hi[:, sl], ones_bd)
        if split:
            acc = acc + _dot(lo[:, sl], ones_bd)
        outs.append(acc)
    return jnp.concatenate(outs, axis=1)


def _swap_pair_halves(x, first_half):
    w = x.shape[1]
    return jnp.where(first_half, pltpu.roll(x, w - HEAD_DIM, axis=1), pltpu.roll(x, HEAD_DIM, axis=1))


def _rwkv_kernel(zm_ref, zl_ref, mum_ref, mul_ref, w0_ref, w2_ref, a0_ref, a2_ref, g2_ref,
                 kk_ref, ka_ref, rk_ref, lw_ref, lb_ref,
                 o_ref,
                 st_ref, carm_ref, carl_ref,
                 a_s, r_s, b_s, k_s, bh_s, kh_s, vsw_s, pe_s, y_s, y0_s,
                 *, tt, width, dlp, alp):
    c_len = RWKV_CHUNK
    n = HEAD_DIM
    pairs = width // LANES
    pp = RWKV_PAIRS_PER_STEP if pairs % RWKV_PAIRS_PER_STEP == 0 else 1

    @pl.when(pl.program_id(1) == 0)
    def _():
        st_ref[...] = jnp.zeros_like(st_ref)
        carm_ref[...] = jnp.zeros_like(carm_ref)
        carl_ref[...] = jnp.zeros_like(carl_ref)

    row = lax.broadcasted_iota(jnp.int32, (tt, 1), 0)

    def shift_lerp(x, car_ref, mu):
        xs = pltpu.roll(x, 1, axis=0)
        xs = jnp.where(row == 0, car_ref[...], xs)
        car_ref[...] = x[tt - 1:tt, :]
        return x + (xs - x) * mu

    zm = shift_lerp(zm_ref[...], carm_ref, mum_ref[...])
    zl = shift_lerp(zl_ref[...], carl_ref, mul_ref[...])
    r = zm[:, :width]
    k = zm[:, width:2 * width]
    v = zm[:, 2 * width:]
    xw = zl[:, :dlp]
    xa = zl[:, dlp:dlp + alp]
    xg = zl[:, dlp + alp:]

    wr = w0_ref[...] + _dot(_bf(jnp.tanh(xw)), w2_ref[...])
    wlog = -math.exp(-0.5) * _sigmoid(wr)
    a = _sigmoid(a0_ref[...] + _dot(_bf(xa), a2_ref[...]))
    g = _dot(_bf(_sigmoid(xg)), g2_ref[...])

    ri = lax.broadcasted_iota(jnp.int32, (MXU_DIM, MXU_DIM), 0)
    ci = lax.broadcasted_iota(jnp.int32, (MXU_DIM, MXU_DIM), 1)
    shift = int(math.log2(n))
    ones_bd = jnp.where((ri >> shift) == (ci >> shift), 1.0, 0.0).astype(BF16)

    kkv = k * kk_ref[...]
    ssq = _segsum(kkv * kkv, ones_bd, split=True)
    kkn = kkv * lax.rsqrt(jnp.maximum(ssq, 1e-24))
    k2 = k * (1.0 + (a - 1.0) * ka_ref[...])
    beta = kkn * a
    bonus = _segsum(r * k2 * rk_ref[...], ones_bd) * v

    rt_i = lax.broadcasted_iota(jnp.int32, (tt, tt), 0)
    ct_i = lax.broadcasted_iota(jnp.int32, (tt, tt), 1)
    cshift = int(math.log2(c_len))
    same = (rt_i >> cshift) == (ct_i >> cshift)
    lt_bd = jnp.where(same & (ct_i <= rt_i), 1.0, 0.0).astype(BF16)
    le_bd = jnp.where(same, 1.0, 0.0).astype(BF16)
    w_hi = _bf(wlog)
    w_lo = _bf(wlog - w_hi.astype(F32))
    cum = _dot(lt_bd, w_hi) + _dot(lt_bd, w_lo)
    cend = _dot(le_bd, w_hi) + _dot(le_bd, w_lo)

    first_half = (lax.broadcasted_iota(jnp.int32, (1, width), 1) & (LANES - 1)) < n
    inv = jnp.exp(-cum)
    dend = jnp.exp(cend - cum)
    a_s[...] = _bf(-kkn * jnp.exp(cum - wlog))
    r_s[...] = _bf(r * jnp.exp(cum))
    b_s[...] = _bf(beta * inv)
    k_s[...] = _bf(k2 * inv)
    bh_s[...] = _bf(beta * dend)
    kh_s[...] = _bf(k2 * dend)
    vsw_s[...] = _bf(_swap_pair_halves(v, first_half))
    pe_s[...] = jnp.exp(cend)

    lane = lax.broadcasted_iota(jnp.int32, (c_len, LANES), 1)
    rowi = lax.broadcasted_iota(jnp.int32, (c_len, LANES), 0)
    m0f = lane < n
    m0b = jnp.where(m0f, 1.0, 0.0).astype(BF16)
    m1b = jnp.where(m0f, 0.0, 1.0).astype(BF16)
    sidx = lane & (n - 1)
    strict2 = sidx < rowi
    incl2 = sidx <= rowi
    r2 = lax.broadcasted_iota(jnp.int32, (LANES, LANES), 0)
    l2 = lax.broadcasted_iota(jnp.int32, (LANES, LANES), 1)
    bd = (r2 < n) == (l2 < n)
    eye2 = r2 == l2
    zer = jnp.zeros((c_len, LANES), BF16)
    n_mid_rounds = int(math.log2(c_len)) - 2

    n_chunks = tt // c_len
    hpp = LANES // n

    def group_body(gidx, carry):
        lanes = [pl.ds(pl.multiple_of((gidx * pp + pi) * LANES, LANES), LANES) for pi in range(pp)]
        slabs = [(pi, c) for pi in range(pp) for c in range(n_chunks)]
        chains = [(si, q) for si in range(len(slabs)) for q in range(hpp)]
        rows = [slice(c * c_len, (c + 1) * c_len) for c in range(n_chunks)]

        a_p = [a_s[rows[c], lanes[pi]] for pi, c in slabs]
        r_p = [r_s[rows[c], lanes[pi]] for pi, c in slabs]
        bh_p = [bh_s[rows[c], lanes[pi]] for pi, c in slabs]
        kh_p = [kh_s[rows[c], lanes[pi]] for pi, c in slabs]
        vsw_p = [vsw_s[rows[c], lanes[pi]] for pi, c in slabs]
        bk = [jnp.concatenate([b_s[rows[c], lanes[pi]], k_s[rows[c], lanes[pi]]], axis=0) for pi, c in slabs]
        msk = [(m0b, m1b) if q == 0 else (m1b, m0b) for _, q in chains]

        aq = [a_p[si] * msk[ci][0] for ci, (si, q) in enumerate(chains)]
        vq = [vsw_p[si] * msk[ci][1] for ci, (si, q) in enumerate(chains)]
        sc = [_dot_nt(jnp.concatenate([aq[ci], r_p[si] * msk[ci][0]], axis=0), bk[si])
              for ci, (si, q) in enumerate(chains)]
        top = [_bf(jnp.where(strict2, x[:c_len], 0.0)) for x in sc]
        bot = [_bf(jnp.where(incl2, x[c_len:], 0.0)) for x in sc]
        z = [aq[ci].astype(F32) + _dot(top[ci], jnp.concatenate([zer, vq[ci]], axis=0))
             for ci in range(len(chains))]
        t = [_dot(top[ci][:, :n], jnp.concatenate([_bf(z[ci]), top[ci]], axis=1)) for ci in range(len(chains))]
        z = [z[ci] + t[ci][:, :LANES] for ci in range(len(chains))]
        pb = [_bf(t[ci][:, LANES:]) for ci in range(len(chains))]
        for _ in range(n_mid_rounds):
            t = [_dot(pb[ci][:, :n], jnp.concatenate([_bf(z[ci]), pb[ci]], axis=1)) for ci in range(len(chains))]
            z = [z[ci] + t[ci][:, :LANES] for ci in range(len(chains))]
            pb = [_bf(t[ci][:, LANES:]) for ci in range(len(chains))]
        z = [z[ci] + _dot(pb[ci][:, :n], _bf(z[ci])) for ci in range(len(chains))]
        qv = [jnp.concatenate([_bf(z[ci]), vq[ci]], axis=0) for ci in range(len(chains))]
        ws = [_dot(bot[ci], qv[ci]) for ci in range(len(chains))]
        gq = [_dot_tn(qv[ci], jnp.concatenate([bh_p[si] * msk[ci][0], kh_p[si] * msk[ci][0]], axis=0))
              for ci, (si, q) in enumerate(chains)]

        rhat, m_bd, n_bd = [], [], []
        for si, (pi, c) in enumerate(slabs):
            w0, w1 = ws[si * hpp], ws[si * hpp + 1]
            mn = gq[si * hpp] + gq[si * hpp + 1]
            rhat.append(_bf(r_p[si].astype(F32) + jnp.where(m0f, w0, w1)))
            y0_s[rows[c], lanes[pi]] = jnp.where(m0f, w1, w0)
            pe_p = pe_s[c * c_len:c * c_len + 1, lanes[pi]]
            m_bd.append(_bf(jnp.where(bd, mn, 0.0) + jnp.where(eye2, pe_p, 0.0)))
            n_sw = jnp.where(bd, 0.0, mn)
            n_bd.append(jnp.concatenate([n_sw[n:], n_sw[:n]], axis=0))

        s_bd = [st_ref[gidx * pp + pi] for pi in range(pp)]
        for c in range(n_chunks):
            sb = [_bf(s_bd[pi]) for pi in range(pp)]
            for pi in range(pp):
                y_s[rows[c], lanes[pi]] = _dot_nt(rhat[pi * n_chunks + c], sb[pi])
            s_bd = [_dot(sb[pi], m_bd[pi * n_chunks + c]) + n_bd[pi * n_chunks + c] for pi in range(pp)]
        for pi in range(pp):
            st_ref[gidx * pp + pi] = s_bd[pi]
        return carry

    lax.fori_loop(0, pairs // pp, group_body, 0)

    y = y_s[...] + _swap_pair_halves(y0_s[...], first_half)
    mean = _segsum(y, ones_bd) * (1.0 / n)
    yc = y - mean
    var = _segsum(yc * yc, ones_bd) * (1.0 / n)
    yn = yc * lax.rsqrt(var + LN_X_EPS) * lw_ref[...] + lb_ref[...]
    o_ref[...] = ((yn + bonus) * g).astype(o_ref.dtype)


def _rwkv(z, mu_m, mu_l, w0, w2p, a0, a2p, g2p, k_k, k_a, r_k, lnx_w, lnx_b, *, width, s5w, name):
    b, t, zc = z.shape
    dlp, alp, glp = w2p.shape[0], a2p.shape[0], g2p.shape[0]
    lw = dlp + alp + glp
    tt = _pick(t, (128, 64))
    lora_blk = (3 * width + s5w) // lw
    assert lora_blk * lw == 3 * width + s5w
    pairs = width // LANES

    def row(x):
        return x.reshape(1, -1).astype(F32)

    vec = pl.BlockSpec((1, width), lambda bi, i: (0, 0))
    kern = functools.partial(_rwkv_kernel, tt=tt, width=width, dlp=dlp, alp=alp)
    return pl.pallas_call(
        kern,
        grid=(b, t // tt),
        in_specs=[
            pl.BlockSpec((None, tt, 3 * width), lambda bi, i: (bi, i, 0)),
            pl.BlockSpec((None, tt, lw), lambda bi, i: (bi, i, lora_blk)),
            pl.BlockSpec((1, 3 * width), lambda bi, i: (0, 0)),
            pl.BlockSpec((1, lw), lambda bi, i: (0, 0)),
            vec,
            pl.BlockSpec((dlp, width), lambda bi, i: (0, 0)),
            vec,
            pl.BlockSpec((alp, width), lambda bi, i: (0, 0)),
            pl.BlockSpec((glp, width), lambda bi, i: (0, 0)),
            vec, vec, vec, vec, vec,
        ],
        out_specs=pl.BlockSpec((None, tt, width), lambda bi, i: (bi, i, 0)),
        out_shape=jax.ShapeDtypeStruct((b, t, width), BF16),
        scratch_shapes=[
            pltpu.VMEM((pairs, LANES, LANES), F32),
            pltpu.VMEM((1, 3 * width), F32),
            pltpu.VMEM((1, lw), F32),
        ] + [pltpu.VMEM((tt, width), BF16)] * 7 + [pltpu.VMEM((tt, width), F32)] * 3,
        compiler_params=_cparams(("parallel", "arbitrary")),
        name=name,
    )(z, z, row(mu_m), row(mu_l), row(w0), w2p, row(a0), a2p, g2p,
      row(k_k), row(k_a), row(r_k), row(lnx_w), row(lnx_b))


def _gelu_tanh(x):
    return x * (0.5 * (1.0 + jnp.tanh(0.7978845608028654 * (x + 0.044715 * (x * x * x)))))


def _s5_kernel(u_ref, bd_ref, wz_ref, gz_ref, m1_ref, m2_ref, o_ref, bdw_s, xp_s, *, nlev):
    lc, nc, _ = u_ref.shape
    gpb, ch, sp2 = wz_ref.shape[1:]
    half = sp2 // 2

    bdw_s[...] = jnp.zeros_like(bdw_s)
    z = None
    for s in range(lc):
        buf = bdw_s.at[s % 2]
        for g in range(gpb):
            buf[g * ch:(g + 1) * ch, g * sp2:(g + 1) * sp2] = wz_ref[s, g]
        d = _dot(u_ref[s], buf[...])
        z = d if z is None else z + d

    cidx = lax.broadcasted_iota(jnp.int32, (nc, 1), 0)
    gs = range(gpb)
    x = [z[:, g * sp2:(g + 1) * sp2] for g in gs]

    def scan_level(x, lev):
        sh = 1 << lev
        xs = [jnp.where(cidx >= sh, pltpu.roll(x[g], sh, axis=0), 0.0) for g in gs]
        xsw = [pltpu.roll(xs[g], half, axis=1) for g in gs]
        return [x[g] + m1_ref[lev:lev + 1, g * sp2:(g + 1) * sp2] * xs[g]
                + m2_ref[lev:lev + 1, g * sp2:(g + 1) * sp2] * xsw[g] for g in gs]

    y_intra = [None] * lc
    for i, t in enumerate(reversed(range(lc))):
        acc = _dot(u_ref[0], bd_ref[t])
        for s in range(1, t + 1):
            acc = acc + _dot(u_ref[s], bd_ref[t - s])
        y_intra[t] = acc
        if i < nlev:
            x = scan_level(x, i)
    for lev in range(lc, nlev):
        x = scan_level(x, lev)
    for g in gs:
        xp_s[:, g * sp2:(g + 1) * sp2] = _bf(jnp.where(cidx >= 1, pltpu.roll(x[g], 1, axis=0), 0.0))

    xp = xp_s[...]
    for t in range(lc):
        buf = bdw_s.at[t % 2]
        for g in range(gpb):
            buf[g * ch:(g + 1) * ch, g * sp2:(g + 1) * sp2] = gz_ref[t, g]
        y = y_intra[t] + _dot_nt(xp, buf[...])
        o_ref[t] = _gelu_tanh(y).astype(o_ref.dtype)


def _s5_core(u_p, bd, wz, gz, m1, m2, *, name):
    bsz, lc, nc, s5w = u_p.shape
    nblk, _, gpb, ch, sp2 = wz.shape
    nlev = m1.shape[1]
    return pl.pallas_call(
        functools.partial(_s5_kernel, nlev=nlev),
        grid=(bsz, nblk),
        in_specs=[
            pl.BlockSpec((None, lc, nc, MXU_DIM), lambda b, j: (b, 0, 0, j)),
            pl.BlockSpec((None, lc, MXU_DIM, MXU_DIM), lambda b, j: (j, 0, 0, 0)),
            pl.BlockSpec((None, lc, gpb, ch, sp2), lambda b, j: (j, 0, 0, 0, 0)),
            pl.BlockSpec((None, lc, gpb, ch, sp2), lambda b, j: (j, 0, 0, 0, 0)),
            pl.BlockSpec((None, nlev, gpb * sp2), lambda b, j: (j, 0, 0)),
            pl.BlockSpec((None, nlev, gpb * sp2), lambda b, j: (j, 0, 0)),
        ],
        out_specs=pl.BlockSpec((None, lc, nc, MXU_DIM), lambda b, j: (b, 0, 0, j)),
        out_shape=jax.ShapeDtypeStruct((bsz, lc, nc, s5w), BF16),
        scratch_shapes=[pltpu.VMEM((2, MXU_DIM, gpb * sp2), BF16), pltpu.VMEM((nc, gpb * sp2), BF16)],
        compiler_params=_cparams(("parallel", "arbitrary")),
        name=name,
    )(u_p, bd, wz, gz, m1, m2)


def _s5_operators(lam_re, lam_im, log_step, b_re, b_im, c_re, c_im, d_skip, *, nc):
    groups, sp = lam_re.shape
    ch = b_re.shape[2]
    lc = S5_CHUNK
    lr = lam_re.astype(F32)
    li = lam_im.astype(F32)
    dt = jnp.exp(log_step.astype(F32))[:, None]

    def apow(nsteps):
        nsteps = jnp.asarray(nsteps, F32)[:, None, None]
        mag = jnp.exp(lr * dt * nsteps)
        ang = li * dt * nsteps
        return mag * jnp.cos(ang), mag * jnp.sin(ang)

    ar, ai = apow(jnp.arange(lc + 1))
    den = lr * lr + li * li
    q_re = ((ar[1] - 1.0) * lr + ai[1] * li) / den
    q_im = (ai[1] * lr - (ar[1] - 1.0) * li) / den
    bb_re = q_re[:, :, None] * b_re - q_im[:, :, None] * b_im
    bb_im = q_re[:, :, None] * b_im + q_im[:, :, None] * b_re
    ce_re = c_re[None] * ar[:, :, None, :] - c_im[None] * ai[:, :, None, :]
    ce_im = c_re[None] * ai[:, :, None, :] + c_im[None] * ar[:, :, None, :]
    gpb = MXU_DIM // ch
    nblk = groups // gpb
    taps = jnp.sum(ce_re[:lc, :, None, :, :] * jnp.transpose(bb_re, (0, 2, 1))[None, :, :, None, :]
                   - ce_im[:lc, :, None, :, :] * jnp.transpose(bb_im, (0, 2, 1))[None, :, :, None, :], axis=-1)
    skip = jnp.eye(ch, dtype=F32)[None] * d_skip.reshape(groups, 1, ch).astype(F32)
    taps = taps.at[0].add(skip)
    taps = jnp.transpose(taps.reshape(lc, nblk, gpb, ch, ch), (1, 0, 2, 3, 4))
    same_group = jnp.eye(gpb, dtype=jnp.bool_)[None, None, :, None, :, None]
    bd = jnp.where(same_group, _bf(taps)[:, :, :, :, None, :], jnp.zeros((), BF16))
    bd = bd.reshape(nblk, lc, MXU_DIM, MXU_DIM)
    arr = ar[lc - 1 - jnp.arange(lc)]
    air = ai[lc - 1 - jnp.arange(lc)]
    w_re = arr[:, :, :, None] * bb_re[None] - air[:, :, :, None] * bb_im[None]
    w_im = arr[:, :, :, None] * bb_im[None] + air[:, :, :, None] * bb_re[None]
    wz = jnp.transpose(jnp.concatenate([w_re, w_im], axis=2), (1, 0, 3, 2))
    wz = jnp.transpose(wz.reshape(nblk, gpb, lc, ch, 2 * sp), (0, 2, 1, 3, 4))
    gz = jnp.transpose(jnp.concatenate([ce_re[1:], -ce_im[1:]], axis=3), (1, 0, 2, 3))
    gz = jnp.transpose(gz.reshape(nblk, gpb, lc, ch, 2 * sp), (0, 2, 1, 3, 4))
    nlev = max(1, int(math.ceil(math.log2(nc))))
    mr, mi = apow(lc * (2.0 ** jnp.arange(nlev)))

    def per_block(a):
        return jnp.transpose(a.reshape(nlev, nblk, gpb * 2 * sp), (1, 0, 2))

    m1 = per_block(jnp.concatenate([mr, mr], axis=2))
    m2 = per_block(jnp.concatenate([-mi, mi], axis=2))
    return bd, _bf(wz), _bf(gz), m1, m2


def _pad_rows(w, to):
    return jnp.pad(w, ((0, to - w.shape[0]), (0, 0)))


def _round_up(x, m):
    return (x + m - 1) // m * m


def kernel(x, p, g_mix_pre, w_in, mu, w0, w2, a0, a2, g2, k_k, k_a, r_k, lnx_w, lnx_b, lam_re, lam_im, log_step, b_re, b_im, c_re, c_im, d_skip, w_glu, b_glu, w_out, g_mix_post, g_ffn_pre, w_ff1, w_ff2, g_ffn_post, w_ple, g_ple_gate, w_ple_gate, g_ple_post):
    bsz, seq, d = x.shape
    width = w0.shape[-1]
    s5w = w_glu.shape[-1]
    dl, al, gl = w2.shape[1], a2.shape[1], g2.shape[1]
    dlp, alp, glp = _round_up(dl, LANES), _round_up(al, LANES), _round_up(gl, LANES)
    nc = seq // S5_CHUNK
    m = bsz * seq
    o_wl = 3 * width
    o_al = o_wl + dl
    o_gl = o_al + al
    o_s5 = o_gl + gl

    def pad_cols(wm, to):
        return jnp.pad(wm, ((0, 0), (0, to - wm.shape[1])))

    def layer(carry, lp):
        h, hn = carry
        (p_i, g_next_i, wi, mu_i, w0_i, w2_i, a0_i, a2_i, g2_i, k_k_i, k_a_i, r_k_i, lnx_w_i, lnx_b_i,
         lam_re_i, lam_im_i, log_step_i, b_re_i, b_im_i, c_re_i, c_im_i, d_skip_i, w_glu_i, b_glu_i, w_out_i,
         g_mix_post_i, g_ffn_pre_i, w_ff1_i, w_ff2_i, g_ffn_post_i, w_ple_i, g_ple_gate_i, w_ple_gate_i,
         g_ple_post_i) = lp
        w_perm = _bf(jnp.concatenate([
            wi[:, :o_wl], wi[:, o_s5:],
            pad_cols(wi[:, o_wl:o_al], dlp), pad_cols(wi[:, o_al:o_gl], alp), pad_cols(wi[:, o_gl:o_s5], glp)],
            axis=1))
        mu_m = mu_i[:o_wl]
        mu_l = jnp.concatenate([jnp.pad(mu_i[o_wl:o_al], (0, dlp - dl)), jnp.pad(mu_i[o_al:o_gl], (0, alp - al)),
                                jnp.pad(mu_i[o_gl:o_s5], (0, glp - gl))])

        z = _mm_rows(hn, w_perm, epilogue="none", out_dtype=F32, name="in_proj")
        z3 = z.reshape(bsz, seq, z.shape[1])
        y_rwkv = _rwkv(z3, mu_m, mu_l, w0_i, _bf(_pad_rows(w2_i, dlp)), a0_i, _bf(_pad_rows(a2_i, alp)),
                       _bf(_pad_rows(g2_i, glp)), k_k_i, k_a_i, r_k_i, lnx_w_i, lnx_b_i,
                       width=width, s5w=s5w, name="rwkv")

        u = _bf(z3[:, :, o_wl:o_wl + s5w])
        u_p = jnp.transpose(u.reshape(bsz, nc, S5_CHUNK, s5w), (0, 2, 1, 3))
        ops = _s5_operators(lam_re_i, lam_im_i, log_step_i, b_re_i, b_im_i, c_re_i, c_im_i, d_skip_i, nc=nc)
        y_p = _s5_core(u_p, *ops, name="s5")
        y_s = jnp.transpose(y_p, (0, 2, 1, 3)).reshape(m, s5w)
        y_s5 = _mm_rows(y_s, _bf(w_glu_i), epilogue="glu", out_dtype=BF16, extra=(y_s, b_glu_i), name="glu")

        if width == s5w:
            mix_in = [y_rwkv.reshape(m, width), y_s5]
        else:
            mix_in = [jnp.concatenate([y_rwkv.reshape(m, width), y_s5], axis=1)]
        mixed = _mm_kt(mix_in, _bf(w_out_i), name="out_proj")
        h, hn = _resid_norm(h, mixed, g_mix_post_i, g_ffn_pre_i, name="mix_post")

        hid = _mm_rows(hn, _bf(w_ff1_i), epilogue="relu2", out_dtype=BF16, name="ff1")
        f = _mm_kt([hid], _bf(w_ff2_i), name="ff2")
        h, hn = _resid_norm(h, f, g_ffn_post_i, g_ple_gate_i, name="ffn_post")

        ge = _mm_rows(hn, _bf(w_ple_gate_i), epilogue="ple", out_dtype=BF16,
                      extra=(_bf(p_i.reshape(m, -1)), _bf(w_ple_i)), name="ple")
        h, hn = _resid_norm(h, ge, g_ple_post_i, g_next_i, name="ple_post")
        return (h, hn), None

    g_next = jnp.roll(g_mix_pre, -1, axis=0)
    stacked = (p, g_next, w_in, mu, w0, w2, a0, a2, g2, k_k, k_a, r_k, lnx_w, lnx_b, lam_re, lam_im, log_step,
               b_re, b_im, c_re, c_im, d_skip, w_glu, b_glu, w_out, g_mix_post, g_ffn_pre, w_ff1, w_ff2, g_ffn_post,
               w_ple, g_ple_gate, w_ple_gate, g_ple_post)
    h0 = x.reshape(m, d)
    hn0 = _norm_cast(h0, g_mix_pre[0], name="pre_norm")
    carry = (h0, hn0)
    for i in range(w_in.shape[0]):
        carry, _ = layer(carry, tuple(a[i] for a in stacked))
    return carry[0].reshape(bsz, seq, d)
```
